```python
import jax
import jax.numpy as jnp
from jax import lax
import numpy as np

D_MODEL = 2048
BATCH = 2
SEQ = 4096
DEPTH = 4
DEC_BATCH = 8
DEC_SEQ = 4
PAST_LEN = 16384
PAGE_SIZE = 128

N_EVEN = (DEPTH + 1) // 2
N_ODD = DEPTH // 2
N_MOD = 6
A_HEAD = 64
A_HEADS = 16
A_WIDTH = A_HEADS * A_HEAD
DECAY_LORA = 96
AAA_LORA = 96
GATE_LORA = 256
A_GN_EPS = 64e-5
B_WIDTH = 1024
B_CONV = 31
AB_IN = 3 * A_WIDTH + 2 * B_WIDTH
C_WINDOWS = (128, 512, 2048)
C_DILATIONS = (1, 4, 16)
C_GROUPS = len(C_WINDOWS)
C_HEADS = 8
C_HEAD = 128
C_WIDTH = C_HEADS * C_HEAD
C_QBLOCK = 128
C_SCALE = C_HEAD ** -0.5
D_FF = 5632
FFN_CONV = 3
RMS_EPS = 1e-6
LN_EPS = 1e-5

kernel_name = 'hybrid_rwkv7_conformer_dilated_attn_decoder_step'


def rmsnorm(x, g):
    xf = x.astype(jnp.float32)
    y = xf * lax.rsqrt(jnp.mean(xf * xf, axis=-1, keepdims=True) + RMS_EPS)
    return (y * g.astype(jnp.float32)).astype(x.dtype)


def layernorm(x, w, b, eps):
    xf = x.astype(jnp.float32)
    xc = xf - jnp.mean(xf, axis=-1, keepdims=True)
    var = jnp.mean(xc * xc, axis=-1, keepdims=True)
    return (xc * lax.rsqrt(var + eps) * w.astype(jnp.float32) + b.astype(jnp.float32)).astype(x.dtype)


def modulate(x, g, shift, scale):
    return rmsnorm(x, g) * (1 + scale) + shift


def causal_dwconv(buf, u, w, b):
    xx = jnp.concatenate([buf.astype(u.dtype), u], axis=1)
    y = lax.conv_general_dilated(xx, w[:, None, :].astype(u.dtype), window_strides=(1,), padding='VALID',
                                 dimension_numbers=('NWC', 'WIO', 'NWC'), feature_group_count=u.shape[-1])
    return y + b.astype(u.dtype), xx[:, xx.shape[1] - (w.shape[0] - 1):]


def wkv7_scan(s0, r, w, k, v, kk, a):
    def step(S, xs):
        r_t, w_t, k_t, v_t, kk_t, a_t = xs
        sa = jnp.einsum('bhvk,bhk->bhv', S, -kk_t)
        S = S * w_t[:, :, None, :] + sa[..., None] * (kk_t * a_t)[:, :, None, :] + v_t[..., None] * k_t[:, :, None, :]
        return S, jnp.einsum('bhvk,bhk->bhv', S, r_t)
    xs = tuple(jnp.moveaxis(t, 1, 0) for t in (r, w, k, v, kk, a))
    S, o = lax.scan(step, s0, xs)
    return jnp.moveaxis(o, 0, 1), S


def rwkv_conv_mixer(h, shift_prev, wkv_prev, convb_prev, W, e):
    f32 = jnp.float32
    Bn, T, _ = h.shape
    dt = h.dtype
    w_in = W['ab_w_in'][e]
    shift_prev = shift_prev.astype(dt)
    h_prev = jnp.concatenate([shift_prev[:, None], h[:, :-1]], axis=1)
    delta = h_prev - h
    p = h @ w_in
    pa = p[..., :3 * A_WIDTH]
    pa_first = (shift_prev @ w_in[:, :3 * A_WIDTH])[:, None]
    pa_prev = jnp.concatenate([pa_first, pa[:, :-1]], axis=1)
    rkv = (pa + (pa_prev - pa) * W['a_mu_rkv'][e]).astype(f32).reshape(Bn, T, 3, A_HEADS, A_HEAD)
    r, k, v = rkv[:, :, 0], rkv[:, :, 1], rkv[:, :, 2]
    mu = W['a_mu_wag'][e]
    xw = h + delta * mu[0]
    xa = h + delta * mu[1]
    xg = h + delta * mu[2]
    def heads(t):
        return t.astype(f32).reshape(Bn, T, A_HEADS, A_HEAD)
    def per_ch(t):
        return t.astype(f32).reshape(A_HEADS, A_HEAD)
    w_log = -jax.nn.softplus(-heads(W['a_w0'][e] + jnp.tanh(xw @ W['a_w1'][e]) @ W['a_w2'][e])) - 0.5
    decay = jnp.exp(-jnp.exp(w_log))
    a = jax.nn.sigmoid(heads(W['a_a0'][e] + (xa @ W['a_a1'][e]) @ W['a_a2'][e]))
    g = jax.nn.sigmoid(xg @ W['a_g1'][e]) @ W['a_g2'][e]
    kk = k * per_ch(W['a_k_k'][e])
    kk = kk * lax.rsqrt(jnp.maximum(jnp.sum(kk * kk, axis=-1, keepdims=True), 1e-24))
    k = k * (1 + (a - 1) * per_ch(W['a_k_a'][e]))
    o, S = wkv7_scan(wkv_prev.astype(f32), r, decay, k, v, kk, a)
    o = layernorm(o, per_ch(W['a_ln_w'][e]), per_ch(W['a_ln_b'][e]), A_GN_EPS)
    o = o + jnp.sum(r * k * W['a_r_k'][e].astype(f32), axis=-1, keepdims=True) * v
    o_a = o.reshape(Bn, T, A_WIDTH).astype(dt) * g
    pb = p[..., 3 * A_WIDTH:]
    u = pb[..., :B_WIDTH] * jax.nn.sigmoid(pb[..., B_WIDTH:])
    ub, convb_new = causal_dwconv(convb_prev, u, W['b_conv_w'][e], W['b_conv_b'][e])
    o_b = jax.nn.silu(layernorm(ub, W['b_ln_w'][e], W['b_ln_b'][e], LN_EPS))
    y = jnp.concatenate([o_a, o_b.astype(dt)], axis=-1) @ W['ab_w_out'][e]
    return y, h[:, -1], S, convb_new


def dilated_attention(q, kc, vc, q_off, dil, n_keys):
    Bn, T, H, hd = q.shape
    L = kc.shape[1]
    qb = min(T, C_QBLOCK)
    nb = -(-T // qb)
    qp = jnp.pad(q, ((0, 0), (0, nb * qb - T), (0, 0), (0, 0)))
    qblocks = jnp.moveaxis(qp.reshape(Bn, nb, qb, H, hd), 1, 0)
    offs = dil * jnp.arange(n_keys)
    def block(args):
        qblk, bi = args
        t = q_off + bi * qb + jnp.arange(qb)
        pos = t[:, None] - offs[None, :]
        valid = pos >= 0
        idx = jnp.clip(pos, 0, L - 1)
        kg = jnp.take(kc, idx, axis=1)
        vg = jnp.take(vc, idx, axis=1)
        s = jnp.einsum('bqhd,bqjhd->bhqj', qblk, kg).astype(jnp.float32) * C_SCALE
        s = jnp.where(valid[None, None], s, -jnp.inf)
        lse = jax.nn.logsumexp(s, axis=-1)
        p = jnp.exp(s - lse[..., None]).astype(vg.dtype)
        return jnp.einsum('bhqj,bqjhd->bqhd', p, vg), lse
    o, lse = lax.map(block, (qblocks, jnp.arange(nb)))
    o = jnp.moveaxis(o, 0, 1).reshape(Bn, nb * qb, H, hd)[:, :T]
    lse = jnp.transpose(lse, (1, 0, 3, 2)).reshape(Bn, nb * qb, H)[:, :T]
    return o, lse


def dilated_mixer(h, kv_bufs, w_qkv, w_out):
    Bn, T, _ = h.shape
    p = (h @ w_qkv).reshape(Bn, T, 3, C_GROUPS, C_HEADS, C_HEAD)
    outs, lses, rows = [], [], []
    for gi in range(C_GROUPS):
        win, dil = C_WINDOWS[gi], C_DILATIONS[gi]
        q, k, v = p[:, :, 0, gi], p[:, :, 1, gi], p[:, :, 2, gi]
        buf = kv_bufs[gi].astype(h.dtype)
        kc = jnp.concatenate([buf[0], k], axis=1)
        vc = jnp.concatenate([buf[1], v], axis=1)
        o, lse = dilated_attention(q, kc, vc, buf.shape[2], dil, win // dil + 1)
        outs.append(o.astype(jnp.float32))
        lses.append(lse)
        keep = min(win, T)
        rows.append(jnp.stack([k[:, T - keep:], v[:, T - keep:]]))
    alpha = jax.nn.softmax(jnp.stack(lses), axis=0)
    o = jnp.sum(alpha[..., None] * jnp.stack(outs), axis=0)
    y = o.reshape(Bn, T, C_WIDTH).astype(h.dtype) @ w_out
    return y, rows


def conv_ffn(h, buf, W, l):
    gate = h @ W['ffn_w_gate'][l]
    up = h @ W['ffn_w_up'][l]
    gc, buf_new = causal_dwconv(buf, gate, W['ffn_conv_w'][l], W['ffn_conv_b'][l])
    return (jax.nn.silu(gc) * up) @ W['ffn_w_down'][l], buf_new


def trunk(x, c, shift_s, wkv_s, convb_s, ffn_s, kv_s, W):
    Bn = x.shape[0]
    new_shift, new_wkv, new_convb, new_ffn = [], [], [], []
    new_kv = [[] for _ in C_WINDOWS]
    for l in range(DEPTH):
        mod = (c @ W['w_mod'][l] + W['b_mod'][l]).reshape(Bn, N_MOD, 1, D_MODEL)
        h = modulate(x, W['g_pre_mix'][l], mod[:, 0], mod[:, 1])
        if l % 2 == 0:
            e = l // 2
            y, s_shift, s_wkv, s_convb = rwkv_conv_mixer(h, shift_s[e], wkv_s[e], convb_s[e], W, e)
            new_shift.append(s_shift)
            new_wkv.append(s_wkv)
            new_convb.append(s_convb)
        else:
            o = l // 2
            y, rows = dilated_mixer(h, tuple(kv[o] for kv in kv_s), W['attn_w_qkv'][o], W['attn_w_out'][o])
            for gi in range(C_GROUPS):
                new_kv[gi].append(rows[gi])
        x = x + mod[:, 2] * rmsnorm(y, W['g_post_mix'][l])
        h = modulate(x, W['g_pre_ffn'][l], mod[:, 3], mod[:, 4])
        y, s_ffn = conv_ffn(h, ffn_s[l], W, l)
        new_ffn.append(s_ffn)
        x = x + mod[:, 5] * rmsnorm(y, W['g_post_ffn'][l])
    return (x, jnp.stack(new_shift), jnp.stack(new_wkv), jnp.stack(new_convb), jnp.stack(new_ffn),
            jnp.stack(new_kv[0]), jnp.stack(new_kv[1]), jnp.stack(new_kv[2]))


def setup_inputs(seed: int = 0) -> dict:
    key = jax.random.key(seed)
    ks = iter(jax.random.split(key, 64))
    f32 = jnp.float32
    D = D_MODEL
    def nrm(shape, scale=1.0):
        return jax.random.normal(next(ks), shape, f32) * scale
    def unif(shape, lo, hi):
        return jax.random.uniform(next(ks), shape, f32, lo, hi)
    def gain(shape):
        return 1.0 + nrm(shape, 0.02)
    kv_len = [min(w, PAST_LEN) for w in C_WINDOWS]
    return {
        'x_prompt': nrm((BATCH, SEQ, D)),
        'x_sample': nrm((DEC_BATCH, DEC_SEQ, D)),
        'state_shift': nrm((N_EVEN, DEC_BATCH, D)),
        'state_wkv': nrm((N_EVEN, DEC_BATCH, A_HEADS, A_HEAD, A_HEAD), 0.5),
        'state_conv_b': nrm((N_EVEN, DEC_BATCH, B_CONV - 1, B_WIDTH), 0.5),
        'state_ffn': nrm((DEPTH, DEC_BATCH, FFN_CONV - 1, D_FF)),
        'cache_kv_w128': nrm((N_ODD, 2, DEC_BATCH, kv_len[0], C_HEADS, C_HEAD)),
        'cache_kv_w512': nrm((N_ODD, 2, DEC_BATCH, kv_len[1], C_HEADS, C_HEAD)),
        'cache_kv_w2048': nrm((N_ODD, 2, DEC_BATCH, kv_len[2], C_HEADS, C_HEAD)),
        'c_prompt': nrm((BATCH, D)),
        'c_sample': nrm((DEC_BATCH, D)),
        'w_mod': nrm((DEPTH, D, N_MOD * D), 0.5 * D ** -0.5),
        'b_mod': nrm((DEPTH, N_MOD * D), 0.02),
        'g_pre_mix': gain((DEPTH, D)),
        'g_post_mix': gain((DEPTH, D)),
        'g_pre_ffn': gain((DEPTH, D)),
        'g_post_ffn': gain((DEPTH, D)),
        'ab_w_in': nrm((N_EVEN, D, AB_IN), D ** -0.5),
        'a_mu_rkv': unif((N_EVEN, 3 * A_WIDTH), 0.0, 1.0),
        'a_mu_wag': unif((N_EVEN, 3, D), 0.0, 1.0),
        'a_w0': unif((N_EVEN, A_WIDTH), -6.0, 1.0),
        'a_w1': nrm((N_EVEN, D, DECAY_LORA), D ** -0.5),
        'a_w2': nrm((N_EVEN, DECAY_LORA, A_WIDTH), 0.5 * DECAY_LORA ** -0.5),
        'a_a0': nrm((N_EVEN, A_WIDTH), 0.1),
        'a_a1': nrm((N_EVEN, D, AAA_LORA), D ** -0.5),
        'a_a2': nrm((N_EVEN, AAA_LORA, A_WIDTH), 0.5 * AAA_LORA ** -0.5),
        'a_g1': nrm((N_EVEN, D, GATE_LORA), D ** -0.5),
        'a_g2': nrm((N_EVEN, GATE_LORA, A_WIDTH), GATE_LORA ** -0.5),
        'a_k_k': 0.85 + nrm((N_EVEN, A_WIDTH), 0.05),
        'a_k_a': 1.0 + nrm((N_EVEN, A_WIDTH), 0.05),
        'a_r_k': nrm((N_EVEN, A_HEADS, A_HEAD), 0.1),
        'a_ln_w': gain((N_EVEN, A_WIDTH)),
        'a_ln_b': nrm((N_EVEN, A_WIDTH), 0.02),
        'b_conv_w': nrm((N_EVEN, B_CONV, B_WIDTH), B_CONV ** -0.5),
        'b_conv_b': nrm((N_EVEN, B_WIDTH), 0.02),
        'b_ln_w': gain((N_EVEN, B_WIDTH)),
        'b_ln_b': nrm((N_EVEN, B_WIDTH), 0.02),
        'ab_w_out': nrm((N_EVEN, A_WIDTH + B_WIDTH, D), (A_WIDTH + B_WIDTH) ** -0.5),
        'attn_w_qkv': nrm((N_ODD, D, 3 * C_GROUPS * C_WIDTH), D ** -0.5),
        'attn_w_out': nrm((N_ODD, C_WIDTH, D), C_WIDTH ** -0.5),
        'ffn_w_gate': nrm((DEPTH, D, D_FF), D ** -0.5),
        'ffn_w_up': nrm((DEPTH, D, D_FF), D ** -0.5),
        'ffn_conv_w': nrm((DEPTH, FFN_CONV, D_FF), FFN_CONV ** -0.5),
        'ffn_conv_b': nrm((DEPTH, D_FF), 0.02),
        'ffn_w_down': nrm((DEPTH, D_FF, D), D_FF ** -0.5),
    }


def reference(x_prompt, x_sample, state_shift, state_wkv, state_conv_b, state_ffn,
              cache_kv_w128, cache_kv_w512, cache_kv_w2048, c_prompt, c_sample,
              w_mod, b_mod, g_pre_mix, g_post_mix, g_pre_ffn, g_post_ffn,
              ab_w_in, a_mu_rkv, a_mu_wag, a_w0, a_w1, a_w2, a_a0, a_a1, a_a2, a_g1, a_g2,
              a_k_k, a_k_a, a_r_k, a_ln_w, a_ln_b, b_conv_w, b_conv_b, b_ln_w, b_ln_b, ab_w_out,
              attn_w_qkv, attn_w_out, ffn_w_gate, ffn_w_up, ffn_conv_w, ffn_conv_b, ffn_w_down):
    W = dict(w_mod=w_mod, b_mod=b_mod, g_pre_mix=g_pre_mix, g_post_mix=g_post_mix,
             g_pre_ffn=g_pre_ffn, g_post_ffn=g_post_ffn, ab_w_in=ab_w_in, a_mu_rkv=a_mu_rkv,
             a_mu_wag=a_mu_wag, a_w0=a_w0, a_w1=a_w1, a_w2=a_w2, a_a0=a_a0, a_a1=a_a1, a_a2=a_a2,
             a_g1=a_g1, a_g2=a_g2, a_k_k=a_k_k, a_k_a=a_k_a, a_r_k=a_r_k, a_ln_w=a_ln_w, a_ln_b=a_ln_b,
             b_conv_w=b_conv_w, b_conv_b=b_conv_b, b_ln_w=b_ln_w, b_ln_b=b_ln_b, ab_w_out=ab_w_out,
             attn_w_qkv=attn_w_qkv, attn_w_out=attn_w_out, ffn_w_gate=ffn_w_gate, ffn_w_up=ffn_w_up,
             ffn_conv_w=ffn_conv_w, ffn_conv_b=ffn_conv_b, ffn_w_down=ffn_w_down)
    nbp = x_prompt.shape[0]
    dt = x_prompt.dtype
    zero_kv = tuple(jnp.zeros((N_ODD, 2, nbp, 0, C_HEADS, C_HEAD), dt) for _ in C_WINDOWS)
    (y_prompt, p_shift, p_wkv, p_conv_b, p_ffn, p_kv_w128, p_kv_w512, p_kv_w2048) = trunk(
        x_prompt, c_prompt,
        jnp.zeros((N_EVEN, nbp, D_MODEL), dt),
        jnp.zeros((N_EVEN, nbp, A_HEADS, A_HEAD, A_HEAD), jnp.float32),
        jnp.zeros((N_EVEN, nbp, B_CONV - 1, B_WIDTH), dt),
        jnp.zeros((DEPTH, nbp, FFN_CONV - 1, D_FF), dt),
        zero_kv, W)
    (y_sample, s_shift, s_wkv, s_conv_b, s_ffn, s_kv_w128, s_kv_w512, s_kv_w2048) = trunk(
        x_sample, c_sample, state_shift, state_wkv, state_conv_b, state_ffn,
        (cache_kv_w128, cache_kv_w512, cache_kv_w2048), W)
    return (y_prompt, y_sample, p_shift, p_wkv, p_conv_b, p_ffn, p_kv_w128, p_kv_w512, p_kv_w2048,
            s_shift, s_wkv, s_conv_b, s_ffn, s_kv_w128, s_kv_w512, s_kv_w2048)
```

```python
import functools

import jax
import jax.numpy as jnp
from jax import lax
from jax.experimental import pallas as pl
from jax.experimental.pallas import tpu as pltpu

_BF = jnp.bfloat16
_F32 = jnp.float32

D_MODEL = 2048
DEPTH = 4
N_MOD = 6
A_HEAD = 64
A_HEADS = 16
A_WIDTH = A_HEADS * A_HEAD
A_GN_EPS = 64e-5
B_WIDTH = 1024
B_CONV = 31
C_WINDOWS = (128, 512, 2048)
C_DILATIONS = (1, 4, 16)
C_GROUPS = 3
C_HEADS = 8
C_HEAD = 128
C_WIDTH = C_HEADS * C_HEAD
C_SCALE = C_HEAD ** -0.5
C_NKEYS = 129
D_FF = 5632
RMS_EPS = 1e-6
LN_EPS = 1e-5

SUBLANES = 8
LANES = 128
VMEM_LIMIT = 56 * 1024 * 1024

SCAN_G = 4
SCAN_GL = SCAN_G * A_HEAD
SCAN_NG = A_HEADS // SCAN_G


def _cparams(*sem):
    return pltpu.CompilerParams(dimension_semantics=sem, vmem_limit_bytes=VMEM_LIMIT)


def _dot(a, b):
    return jnp.dot(a.astype(_BF), b.astype(_BF), preferred_element_type=_F32)


def _dot_nt(a, b):
    return lax.dot_general(a.astype(_BF), b.astype(_BF), (((1,), (1,)), ((), ())),
                           preferred_element_type=_F32)


def _dot_tn(a, b):
    return lax.dot_general(a.astype(_BF), b.astype(_BF), (((0,), (0,)), ((), ())),
                           preferred_element_type=_F32)


def _split3(x):
    h1 = x.astype(_BF)
    r1 = x - h1.astype(_F32)
    h2 = r1.astype(_BF)
    h3 = (r1 - h2.astype(_F32)).astype(_BF)
    return h1, h2, h3


def _dot_exact_rhs(x, m_bf):
    h1, h2, h3 = _split3(x)
    return (jnp.dot(h1, m_bf, preferred_element_type=_F32)
            + jnp.dot(h2, m_bf, preferred_element_type=_F32)
            + jnp.dot(h3, m_bf, preferred_element_type=_F32))


def _split2(x):
    hi = x.astype(_BF)
    return hi, (x - hi.astype(_F32)).astype(_BF)


def _dot_hilo(lhs_list, rhs):
    his, los = zip(*[_split2(a) for a in lhs_list])
    rh, rl = _split2(rhs)
    y1 = jnp.dot(jnp.concatenate(his + los, axis=0), rh, preferred_element_type=_F32)
    y2 = jnp.dot(jnp.concatenate(his, axis=0), rl, preferred_element_type=_F32)
    tot = sum(a.shape[0] for a in lhs_list)
    outs, off = [], 0
    for a in lhs_list:
        m = a.shape[0]
        outs.append(y1[off:off + m] + y1[tot + off:tot + off + m] + y2[off:off + m])
        off += m
    return outs


def _sigmoid(x):
    return 1.0 / (1.0 + jnp.exp(-x))


def _silu(x):
    return x * _sigmoid(x)


def _rms(y, g):
    return y * lax.rsqrt(jnp.mean(y * y, axis=-1, keepdims=True) + RMS_EPS) * g


def _mod_kernel(c_ref, w_ref, b_ref, o_ref):
    c = c_ref[...]
    ch = c.astype(_BF)
    cl = (c - ch.astype(_F32)).astype(_BF)
    y = jnp.dot(jnp.concatenate([ch, cl], axis=0), w_ref[...].astype(_BF),
                preferred_element_type=_F32)
    n = c.shape[0]
    o_ref[...] = y[:n] + y[n:] + b_ref[...]


def _mods(c_all, w_mod, b_mod):
    depth, d, n = w_mod.shape
    rows = c_all.shape[0]
    tn = 1024
    return pl.pallas_call(
        _mod_kernel,
        grid=(depth, n // tn),
        in_specs=[pl.BlockSpec((rows, d), lambda l, j: (0, 0)),
                  pl.BlockSpec((None, d, tn), lambda l, j: (l, 0, j)),
                  pl.BlockSpec((None, 1, tn), lambda l, j: (l, 0, j))],
        out_specs=pl.BlockSpec((None, rows, tn), lambda l, j: (l, 0, j)),
        out_shape=jax.ShapeDtypeStruct((depth, rows, n), _F32),
        compiler_params=_cparams("parallel", "parallel"),
        name="mods",
    )(c_all, w_mod, b_mod.reshape(depth, 1, n))


def _premix_kernel(x_ref, g_ref, sh_ref, sc_ref, w_ref, o_ref, h_ref, hs_ref):
    @pl.when(pl.program_id(1) == 0)
    def _():
        h = _rms(x_ref[...], g_ref[...]) * (1.0 + sc_ref[...]) + sh_ref[...]
        h_ref[...] = h
        hs_ref[...] = h.astype(_BF)

    o_ref[...] = jnp.dot(hs_ref[...], w_ref[...].astype(_BF), preferred_element_type=_F32)


def _mod_spec(mod, tm, bpb):
    ms = mod.shape[1]
    d = mod.shape[2]
    if ms == 1:
        return pl.BlockSpec((None, 1, d), lambda i, *_: (i // bpb, 0, 0))
    assert mod.shape[0] == 1 and ms == tm
    return pl.BlockSpec((None, ms, d), lambda i, *_: (0, 0, 0))


def _premix(x, g, shift, scale, w, n_out, tm, tn, bpb):
    r, d = x.shape
    return pl.pallas_call(
        _premix_kernel,
        grid=(r // tm, n_out // tn),
        in_specs=[pl.BlockSpec((tm, d), lambda i, j: (i, 0)),
                  pl.BlockSpec((1, d), lambda i, j: (0, 0)),
                  _mod_spec(shift, tm, bpb), _mod_spec(scale, tm, bpb),
                  pl.BlockSpec((d, tn), lambda i, j: (0, j))],
        out_specs=[pl.BlockSpec((tm, tn), lambda i, j: (i, j)),
                   pl.BlockSpec((tm, d), lambda i, j: (i, 0))],
        out_shape=[jax.ShapeDtypeStruct((r, n_out), _F32), jax.ShapeDtypeStruct((r, d), _F32)],
        scratch_shapes=[pltpu.VMEM((tm, d), _BF)],
        compiler_params=_cparams("parallel", "arbitrary"),
        name="premix_matmul",
    )(x, g.reshape(1, d), shift, scale, w)


def _plain_mm_kernel(a_ref, w_ref, o_ref):
    o_ref[...] = _dot(a_ref[...], w_ref[...])


def _plain_mm(a, w, n_out, tn):
    m, k = a.shape
    return pl.pallas_call(
        _plain_mm_kernel,
        grid=(n_out // tn,),
        in_specs=[pl.BlockSpec((m, k), lambda j: (0, 0)),
                  pl.BlockSpec((k, tn), lambda j: (0, j))],
        out_specs=pl.BlockSpec((m, tn), lambda j: (0, j)),
        out_shape=jax.ShapeDtypeStruct((m, n_out), _F32),
        compiler_params=_cparams("parallel"),
        name="plain_matmul",
    )(a, w)


def _post_kernel(a_ref, w_ref, x_ref, g_ref, gm_ref, o_ref, acc_ref):
    k = pl.program_id(1)

    @pl.when(k == 0)
    def _():
        acc_ref[...] = jnp.zeros_like(acc_ref)

    acc_ref[...] += _dot(a_ref[...], w_ref[...])

    @pl.when(k == pl.num_programs(1) - 1)
    def _():
        o_ref[...] = x_ref[...] + gm_ref[...] * _rms(acc_ref[...], g_ref[...])


def _post_mm(a, w, x, g, gate, tm, tk, bpb):
    r, kdim = a.shape
    d = w.shape[1]
    return pl.pallas_call(
        _post_kernel,
        grid=(r // tm, kdim // tk),
        in_specs=[pl.BlockSpec((tm, tk), lambda i, k: (i, k)),
                  pl.BlockSpec((tk, d), lambda i, k: (k, 0)),
                  pl.BlockSpec((tm, d), lambda i, k: (i, 0)),
                  pl.BlockSpec((1, d), lambda i, k: (0, 0)),
                  _mod_spec(gate, tm, bpb)],
        out_specs=pl.BlockSpec((tm, d), lambda i, k: (i, 0)),
        out_shape=jax.ShapeDtypeStruct((r, d), _F32),
        scratch_shapes=[pltpu.VMEM((tm, d), _F32)],
        compiler_params=_cparams("parallel", "arbitrary"),
        name="post_matmul",
    )(a, w, x, g.reshape(1, d), gate)


def _ffn_kernel(*refs, seq_mode, bpb, seg):
    if seq_mode:
        (x_ref, gpre_ref, sh_ref, sc_ref, wg_ref, wu_ref, cw_ref, cb_ref, wd_ref, gpost_ref,
         gm_ref, o_ref, gt_ref, hs_ref, acc_ref, carry_ref) = refs
    else:
        (x_ref, gpre_ref, sh_ref, sc_ref, wg_ref, wu_ref, cw_ref, cb_ref, wd_ref, gpost_ref,
         gm_ref, p1_ref, p2_ref, o_ref, gt_ref, hs_ref, acc_ref) = refs
    i = pl.program_id(0)
    j = pl.program_id(1)

    @pl.when(j == 0)
    def _():
        h = _rms(x_ref[...], gpre_ref[...]) * (1.0 + sc_ref[...]) + sh_ref[...]
        hs_ref[...] = h.astype(_BF)
        acc_ref[...] = jnp.zeros_like(acc_ref)

    hs = hs_ref[...]
    gate = jnp.dot(hs, wg_ref[...].astype(_BF), preferred_element_type=_F32)
    up = jnp.dot(hs, wu_ref[...].astype(_BF), preferred_element_type=_F32)
    tm = gate.shape[0]
    rows = lax.broadcasted_iota(jnp.int32, gate.shape, 0)
    g1 = pltpu.roll(gate, 1, axis=0)
    g2 = pltpu.roll(gate, 2, axis=0)
    if seq_mode:
        @pl.when(i % bpb == 0)
        def _():
            carry_ref[j] = jnp.zeros((SUBLANES, gate.shape[1]), _F32)

        tail = carry_ref[j]
        m1, m2 = tail[SUBLANES - 1:SUBLANES], tail[SUBLANES - 2:SUBLANES - 1]
        g1 = jnp.where(rows == 0, m1, g1)
        g2 = jnp.where(rows == 0, m2, jnp.where(rows == 1, m1, g2))
        carry_ref[j] = gate[tm - SUBLANES:]
        gt_ref[...] = gate[tm - SUBLANES:]
    else:
        rs = rows % seg
        g1 = jnp.where(rs == 0, p1_ref[...], g1)
        g2 = jnp.where(rs < 2, p2_ref[...], g2)
        gt_ref[...] = gate
    cw = cw_ref[...]
    gc = cw[0:1] * g2 + cw[1:2] * g1 + cw[2:3] * gate + cb_ref[...]
    act = _silu(gc) * up
    acc_ref[...] += jnp.dot(act.astype(_BF), wd_ref[...].astype(_BF), preferred_element_type=_F32)

    @pl.when(j == pl.num_programs(1) - 1)
    def _():
        o_ref[...] = x_ref[...] + gm_ref[...] * _rms(acc_ref[...], gpost_ref[...])


def _ffn(x, gpre, shift, scale, wg, wu, cw, cb, wd, gpost, gate, tm, tf, bpb, prev=None, seg=1):
    r, d = x.shape
    f = wg.shape[1]
    nblk = r // tm
    seq_mode = prev is None
    gt_rows = SUBLANES if seq_mode else tm
    in_specs = [pl.BlockSpec((tm, d), lambda i, j: (i, 0)),
                pl.BlockSpec((1, d), lambda i, j: (0, 0)),
                _mod_spec(shift, tm, bpb), _mod_spec(scale, tm, bpb),
                pl.BlockSpec((d, tf), lambda i, j: (0, j)),
                pl.BlockSpec((d, tf), lambda i, j: (0, j)),
                pl.BlockSpec((cw.shape[0], tf), lambda i, j: (0, j)),
                pl.BlockSpec((1, tf), lambda i, j: (0, j)),
                pl.BlockSpec((tf, d), lambda i, j: (j, 0)),
                pl.BlockSpec((1, d), lambda i, j: (0, 0)),
                _mod_spec(gate, tm, bpb)]
    args = [x, gpre.reshape(1, d), shift, scale, wg, wu, cw, cb.reshape(1, f), wd,
            gpost.reshape(1, d), gate]
    scratch = [pltpu.VMEM((tm, d), _BF), pltpu.VMEM((tm, d), _F32)]
    if seq_mode:
        scratch.append(pltpu.VMEM((f // tf, SUBLANES, tf), _F32))
    else:
        in_specs += [pl.BlockSpec((tm, tf), lambda i, j: (i, j))] * 2
        args += list(prev)
    return pl.pallas_call(
        functools.partial(_ffn_kernel, seq_mode=seq_mode, bpb=bpb, seg=seg),
        grid=(nblk, f // tf),
        in_specs=in_specs,
        out_specs=[pl.BlockSpec((tm, d), lambda i, j: (i, 0)),
                   pl.BlockSpec((None, gt_rows, tf), lambda i, j: (i, 0, j))],
        out_shape=[jax.ShapeDtypeStruct((r, d), _F32),
                   jax.ShapeDtypeStruct((nblk, gt_rows, f), _F32)],
        scratch_shapes=scratch,
        compiler_params=_cparams("arbitrary", "arbitrary"),
        name="conv_ffn",
    )(*args)


def _prev_rows(cur, first_row_of):
    return first_row_of(pltpu.roll(cur, 1, axis=0))


def _rwkv_prep_kernel(*refs, seq_mode, bpb, seg):
    if seq_mode:
        (h_ref, h8_ref, hst_ref, pa_ref, pa8_ref, past_ref, pb1_ref, pb2_ref) = refs[:8]
        rest = refs[8:]
    else:
        (h_ref, hf_ref, pa_ref, paf_ref, pb1_ref, pb2_ref) = refs[:6]
        rest = refs[6:]
    (murkv_ref, muwag_ref, w0_ref, w1_ref, w2_ref, a0_ref, a1_ref, a2_ref, g1_ref, g2_ref,
     kk_ref, ka_ref, rk_ref, ones_ref,
     r_out, lw_out, k_out, v_out, av_out, bv_out, g_out, bon_out, u_out) = rest
    i = pl.program_id(0)
    h = h_ref[...]
    pa = pa_ref[...]
    rows_h = lax.broadcasted_iota(jnp.int32, h.shape, 0)
    rows_p = lax.broadcasted_iota(jnp.int32, pa.shape, 0)
    hp = pltpu.roll(h, 1, axis=0)
    pp = pltpu.roll(pa, 1, axis=0)
    if seq_mode:
        first = i % bpb == 0
        h0 = jnp.where(first, hst_ref[...], h8_ref[SUBLANES - 1:SUBLANES])
        p0 = jnp.where(first, past_ref[...], pa8_ref[SUBLANES - 1:SUBLANES])
        hp = jnp.where(rows_h == 0, h0, hp)
        pp = jnp.where(rows_p == 0, p0, pp)
    else:
        hp = jnp.where(rows_h % seg == 0, hf_ref[...], hp)
        pp = jnp.where(rows_p % seg == 0, paf_ref[...], pp)
    delta = hp - h
    mu = muwag_ref[...]
    xw = h + delta * mu[0:1]
    xa = h + delta * mu[1:2]
    xg = h + delta * mu[2:3]
    zw = w0_ref[...] + _dot(jnp.tanh(_dot(xw, w1_ref[...])), w2_ref[...])
    w_log = -(jnp.maximum(-zw, 0.0) + jnp.log(1.0 + jnp.exp(-jnp.abs(zw)))) - 0.5
    lw_out[...] = -jnp.exp(w_log)
    a = _sigmoid(a0_ref[...] + _dot(_dot(xa, a1_ref[...]), a2_ref[...]))
    g_out[...] = _dot(_sigmoid(_dot(xg, g1_ref[...])), g2_ref[...])
    rkv = pa + (pp - pa) * murkv_ref[...]
    r = rkv[:, :A_WIDTH]
    k = rkv[:, A_WIDTH:2 * A_WIDTH]
    v = rkv[:, 2 * A_WIDTH:]
    ones = ones_ref[...]

    def segsum(x):
        parts = [_dot_exact_rhs(x[:, c * SCAN_GL:(c + 1) * SCAN_GL], ones)
                 for c in range(A_WIDTH // SCAN_GL)]
        return jnp.concatenate(parts, axis=1)

    kk = k * kk_ref[...]
    kk = kk * lax.rsqrt(jnp.maximum(segsum(kk * kk), 1e-24))
    k2 = k * (1.0 + (a - 1.0) * ka_ref[...])
    r_out[...] = r
    k_out[...] = k2
    v_out[...] = v
    av_out[...] = -kk
    bv_out[...] = kk * a
    bon_out[...] = segsum(r * k2 * rk_ref[...]) * v
    u_out[...] = pb1_ref[...] * _sigmoid(pb2_ref[...])


def _head_ones():
    idx = jnp.arange(SCAN_GL) // A_HEAD
    return (idx[:, None] == idx[None, :]).astype(_BF)


def _rwkv_prep(h, p, W, e, tm, bpb, state=None, first=None, seg=1):
    r, d = h.shape
    a3 = 3 * A_WIDTH
    seq_mode = first is None
    row = lambda n: pl.BlockSpec((tm, n), lambda i: (i, 0))
    full = lambda s: pl.BlockSpec(s, lambda i: (0,) * len(s))
    if seq_mode:
        t8 = tm // SUBLANES
        prev8 = lambda n: pl.BlockSpec((SUBLANES, n), lambda i: (jnp.maximum(i * t8 - 1, 0), 0))
        in_specs = [row(d), prev8(d), pl.BlockSpec((None, 1, d), lambda i: (i // bpb, 0, 0)),
                    row(a3), prev8(a3), pl.BlockSpec((None, 1, a3), lambda i: (i // bpb, 0, 0))]
        args = [h, h, state[0], p, p, state[1]]
    else:
        in_specs = [row(d), row(d), row(a3), row(a3)]
        args = [h, first[0], p, first[1]]
    in_specs += [pl.BlockSpec((tm, B_WIDTH), lambda i: (i, a3 // B_WIDTH)),
                 pl.BlockSpec((tm, B_WIDTH), lambda i: (i, a3 // B_WIDTH + 1))]
    args += [p, p]
    vec = lambda x: x.reshape(1, -1)
    lora = lambda n: -(-n // LANES) * LANES
    pad_c = lambda x: _pad_rows(x, lora(x.shape[1]), axis=1)
    pad_r = lambda x: _pad_rows(x, lora(x.shape[0]), axis=0)
    params = [vec(W['a_mu_rkv'][e]), W['a_mu_wag'][e], vec(W['a_w0'][e]), pad_c(W['a_w1'][e]),
              pad_r(W['a_w2'][e]), vec(W['a_a0'][e]), pad_c(W['a_a1'][e]), pad_r(W['a_a2'][e]),
              W['a_g1'][e],
              W['a_g2'][e], vec(W['a_k_k'][e]), vec(W['a_k_a'][e]), vec(W['a_r_k'][e]),
              _head_ones()]
    in_specs += [full(x.shape) for x in params]
    args += params
    out = jax.ShapeDtypeStruct((r, A_WIDTH), _F32)
    return pl.pallas_call(
        functools.partial(_rwkv_prep_kernel, seq_mode=seq_mode, bpb=bpb, seg=seg),
        grid=(r // tm,),
        in_specs=in_specs,
        out_specs=[row(A_WIDTH)] * 9,
        out_shape=[out] * 9,
        compiler_params=_cparams("parallel"),
        name="rwkv_prep",
    )(*args)


def _scan_group(rc, kc, vc, lw, av, bv, zbd, ltri):
    c = rc.shape[0]
    gc = SCAN_G * c
    cum = _dot_exact_rhs_left(ltri, lw)
    cum_ex = cum - lw
    cl = cum[c - 1:c]
    e_dn = jnp.exp(-cum)
    e_cl = jnp.exp(cl - cum)
    rt = rc * jnp.exp(cum)
    at = av * jnp.exp(cum_ex)
    kt = kc * e_dn
    bt = bv * e_dn
    kh = kc * e_cl
    bh = bv * e_cl

    row_h = lax.broadcasted_iota(jnp.int32, (gc, SCAN_GL), 0) // c
    lane_h = lax.broadcasted_iota(jnp.int32, (gc, SCAN_GL), 1) // A_HEAD
    hm_e = row_h == lane_h

    def expand(x):
        return jnp.where(hm_e, jnp.concatenate([x] * SCAN_G, axis=0), 0.0)

    bd_cc = (lax.broadcasted_iota(jnp.int32, (gc, gc), 0) // c
             == lax.broadcasted_iota(jnp.int32, (gc, gc), 1) // c)

    def bdiag(x):
        return jnp.where(bd_cc, jnp.concatenate([x] * SCAN_G, axis=0), 0.0)

    t_idx = lax.broadcasted_iota(jnp.int32, (c, gc), 0)
    j_idx = lax.broadcasted_iota(jnp.int32, (c, gc), 1) % c
    kte = expand(kt)
    bte = expand(bt)
    a_ak = jnp.where(j_idx < t_idx, _dot_nt(at, kte), 0.0)
    a_ab = jnp.where(j_idx < t_idx, _dot_nt(at, bte), 0.0)
    a_rk = jnp.where(j_idx <= t_idx, _dot_nt(rt, kte), 0.0)
    a_rb = jnp.where(j_idx <= t_idx, _dot_nt(rt, bte), 0.0)
    trow = jnp.where(j_idx == t_idx, 1.0, 0.0) + a_ab
    pw = a_ab
    nlev = c.bit_length() - 1
    for lev in range(nlev):
        lhs = ([trow] if lev >= 1 else []) + ([pw] if lev < nlev - 1 else [])
        res = _dot_hilo(lhs, bdiag(pw))
        if lev >= 1:
            trow = trow + res[0]
        if lev < nlev - 1:
            pw = res[-1]
    ve = expand(vc)
    u0 = _dot(trow, expand(_dot(a_ak, ve)))
    ap = _dot(trow, expand(at))
    rp = rt + _dot(a_rb, expand(ap))
    o0 = _dot(a_rk, ve) + _dot(a_rb, expand(u0))
    o = _dot(rp, zbd) + o0
    u = _dot(ap, zbd) + u0
    zc = zbd[0:A_HEAD]
    for h in range(1, SCAN_G):
        zc = zc + zbd[h * A_HEAD:(h + 1) * A_HEAD]
    dg = jnp.where(lax.broadcasted_iota(jnp.int32, (A_HEAD, SCAN_GL), 1) % A_HEAD
                   == lax.broadcasted_iota(jnp.int32, (A_HEAD, SCAN_GL), 0), jnp.exp(cl), 0.0)
    xs = jnp.concatenate([kh, bh, dg], axis=0)
    ys = jnp.concatenate([vc, u, zc], axis=0)
    bd_ll = (lax.broadcasted_iota(jnp.int32, (SCAN_GL, SCAN_GL), 0) // A_HEAD
             == lax.broadcasted_iota(jnp.int32, (SCAN_GL, SCAN_GL), 1) // A_HEAD)
    znew = jnp.where(bd_ll, _dot_tn(xs, ys), 0.0)
    return o, znew


def _dot_exact_rhs_left(m_bf, x):
    h1, h2, h3 = _split3(x)
    return (jnp.dot(m_bf, h1, preferred_element_type=_F32)
            + jnp.dot(m_bf, h2, preferred_element_type=_F32)
            + jnp.dot(m_bf, h3, preferred_element_type=_F32))


def _scan_kernel(r_ref, k_ref, v_ref, lw_ref, av_ref, bv_ref, z0_ref, o_ref, zout_ref, z_ref):
    t = pl.program_id(1)

    @pl.when(t == 0)
    def _():
        z_ref[...] = z0_ref[...]

    c = r_ref.shape[0]
    ltri = (lax.broadcasted_iota(jnp.int32, (c, c), 0)
            >= lax.broadcasted_iota(jnp.int32, (c, c), 1)).astype(_BF)
    for gi in range(SCAN_NG):
        sl = slice(gi * SCAN_GL, (gi + 1) * SCAN_GL)
        o, znew = _scan_group(r_ref[:, sl], k_ref[:, sl], v_ref[:, sl], lw_ref[:, sl],
                              av_ref[:, sl], bv_ref[:, sl], z_ref[gi], ltri)
        o_ref[:, sl] = o
        z_ref[gi] = znew

    @pl.when(t == pl.num_programs(1) - 1)
    def _():
        zout_ref[...] = z_ref[...]


def _wkv_scan(r, k, v, lw, av, bv, z0, chunk):
    b, t, _ = r.shape
    tok = pl.BlockSpec((None, chunk, A_WIDTH), lambda bi, ti: (bi, ti, 0))
    zspec = pl.BlockSpec((None, SCAN_NG, SCAN_GL, SCAN_GL), lambda bi, ti: (bi, 0, 0, 0))
    return pl.pallas_call(
        _scan_kernel,
        grid=(b, t // chunk),
        in_specs=[tok] * 6 + [zspec],
        out_specs=[tok, zspec],
        out_shape=[jax.ShapeDtypeStruct((b, t, A_WIDTH), _F32),
                   jax.ShapeDtypeStruct((b, SCAN_NG, SCAN_GL, SCAN_GL), _F32)],
        scratch_shapes=[pltpu.VMEM((SCAN_NG, SCAN_GL, SCAN_GL), _F32)],
        compiler_params=_cparams("parallel", "arbitrary"),
        name="wkv7_scan",
    )(r, k, v, lw, av, bv, z0)


def _state_to_bd(s):
    b = s.shape[0]
    st = jnp.swapaxes(s, -1, -2).reshape(b, SCAN_NG, SCAN_G, A_HEAD, A_HEAD)
    eye = jnp.eye(SCAN_G, dtype=s.dtype)
    z = st[:, :, :, :, None, :] * eye[None, None, :, None, :, None]
    return z.reshape(b, SCAN_NG, SCAN_GL, SCAN_GL)


def _bd_to_state(z):
    b = z.shape[0]
    z = z.reshape(b, SCAN_NG, SCAN_G, A_HEAD, SCAN_G, A_HEAD)
    idx = jnp.arange(SCAN_G)
    st = z[:, :, idx, :, idx, :]
    st = jnp.moveaxis(st, 0, 2).reshape(b, A_HEADS, A_HEAD, A_HEAD)
    return jnp.swapaxes(st, -1, -2)


def _mix_out_kernel(o_ref, bon_ref, g_ref, lnw_ref, lnb_ref, xm_ref, xh_ref, cw_ref, cb_ref,
                    blw_ref, blb_ref, ones_ref, out_ref, win_ref):
    tm = o_ref.shape[0]
    ones = ones_ref[...]

    def segmean(x):
        parts = [_dot_exact_rhs(x[:, c * SCAN_GL:(c + 1) * SCAN_GL], ones)
                 for c in range(A_WIDTH // SCAN_GL)]
        return jnp.concatenate(parts, axis=1) * (1.0 / A_HEAD)

    o = o_ref[...]
    oc = o - segmean(o)
    var = segmean(oc * oc)
    on = oc * lax.rsqrt(var + A_GN_EPS) * lnw_ref[...] + lnb_ref[...]
    out_ref[:, :A_WIDTH] = (on + bon_ref[...]) * g_ref[...]

    win_ref[:tm] = xm_ref[...]
    win_ref[tm:] = xh_ref[...]
    cols = []
    for c in range(B_WIDTH // LANES):
        cs = slice(c * LANES, (c + 1) * LANES)
        acc = jnp.zeros((tm, LANES), _F32) + cb_ref[:, cs]
        for j in range(B_CONV):
            acc = acc + cw_ref[j:j + 1, cs] * win_ref[j:j + tm, cs]
        cols.append(acc)
    ub = jnp.concatenate(cols, axis=1)
    uc = ub - jnp.mean(ub, axis=-1, keepdims=True)
    uv = jnp.mean(uc * uc, axis=-1, keepdims=True)
    out_ref[:, A_WIDTH:] = _silu(uc * lax.rsqrt(uv + LN_EPS) * blw_ref[...] + blb_ref[...])


_CONV_HALO = 32


def _mix_out(o, bon, g, xx, W, e, tm):
    b, t, _ = o.shape
    tok = pl.BlockSpec((None, tm, A_WIDTH), lambda bi, ti: (bi, ti, 0))
    full = lambda s: pl.BlockSpec(s, lambda bi, ti: (0,) * len(s))
    hb = tm // _CONV_HALO
    vec = lambda x: x.reshape(1, -1)
    params = [vec(W['a_ln_w'][e]), vec(W['a_ln_b'][e])]
    conv = [W['b_conv_w'][e], vec(W['b_conv_b'][e]), vec(W['b_ln_w'][e]), vec(W['b_ln_b'][e]),
            _head_ones()]
    return pl.pallas_call(
        _mix_out_kernel,
        grid=(b, t // tm),
        in_specs=[tok, tok, tok] + [full(x.shape) for x in params]
        + [pl.BlockSpec((None, tm, B_WIDTH), lambda bi, ti: (bi, ti, 0)),
           pl.BlockSpec((None, _CONV_HALO, B_WIDTH), lambda bi, ti: (bi, (ti + 1) * hb, 0))]
        + [full(x.shape) for x in conv],
        out_specs=pl.BlockSpec((None, tm, A_WIDTH + B_WIDTH), lambda bi, ti: (bi, ti, 0)),
        out_shape=jax.ShapeDtypeStruct((b, t, A_WIDTH + B_WIDTH), _F32),
        scratch_shapes=[pltpu.VMEM((tm + _CONV_HALO, B_WIDTH), _F32)],
        compiler_params=_cparams("parallel", "parallel"),
        name="mix_out",
    )(o, bon, g, *params, xx, xx, *conv)


def _attn_seq_kernel(q_ref, kp_ref, kc_ref, vp_ref, vc_ref, o_ref, lse_ref):
    qi = pl.program_id(2)
    tq = q_ref.shape[0]
    row = lax.broadcasted_iota(jnp.int32, (tq, 2 * tq), 0)
    col = lax.broadcasted_iota(jnp.int32, (tq, 2 * tq), 1)
    diff = tq + row - col
    valid = (diff >= 0) & (diff <= C_NKEYS - 1) & ((col >= tq) | (qi > 0))
    for h in range(C_HEADS):
        hs = slice(h * C_HEAD, (h + 1) * C_HEAD)
        q = q_ref[:, hs]
        kcat = jnp.concatenate([kp_ref[:, hs], kc_ref[:, hs]], axis=0)
        vcat = jnp.concatenate([vp_ref[:, hs], vc_ref[:, hs]], axis=0)
        s = _dot_nt(q, kcat) * C_SCALE
        s = jnp.where(valid, s, -jnp.inf)
        m = jnp.max(s, axis=-1, keepdims=True)
        p = jnp.exp(s - m)
        l = jnp.sum(p, axis=-1, keepdims=True)
        o_ref[:, hs] = _dot(p, vcat) / l
        lse_ref[:, hs] = jnp.broadcast_to(m + jnp.log(l), (tq, C_HEAD))


def _attn_seq(qkv, gi, tq):
    b, t, n = qkv.shape
    d = C_DILATIONS[gi]
    ncb = n // C_WIDTH
    x = qkv.reshape(b, t // d, d * n)
    nq = t // d // tq
    cur = lambda s: pl.BlockSpec((None, tq, C_WIDTH),
                                 lambda bi, ri, qi: (bi, qi, ri * ncb + s * C_GROUPS + gi))
    prv = lambda s: pl.BlockSpec((None, tq, C_WIDTH),
                                 lambda bi, ri, qi: (bi, jnp.maximum(qi - 1, 0),
                                                     ri * ncb + s * C_GROUPS + gi))
    ospec = pl.BlockSpec((None, tq, C_WIDTH), lambda bi, ri, qi: (bi, qi, ri))
    oshape = jax.ShapeDtypeStruct((b, t // d, d * C_WIDTH), _F32)
    o, lse = pl.pallas_call(
        _attn_seq_kernel,
        grid=(b, d, nq),
        in_specs=[cur(0), prv(1), cur(1), prv(2), cur(2)],
        out_specs=[ospec, ospec],
        out_shape=[oshape, oshape],
        compiler_params=_cparams("parallel", "parallel", "parallel"),
        name=f"attn_seq_d{d}",
    )(x, x, x, x, x)
    return o.reshape(b, t, C_WIDTH), lse.reshape(b, t, C_WIDTH)


def _attn_combine_kernel(o0, o1, o2, l0, l1, l2, out_ref):
    a0, a1, a2 = l0[...], l1[...], l2[...]
    m = jnp.maximum(jnp.maximum(a0, a1), a2)
    e0, e1, e2 = jnp.exp(a0 - m), jnp.exp(a1 - m), jnp.exp(a2 - m)
    out_ref[...] = (e0 * o0[...] + e1 * o1[...] + e2 * o2[...]) / (e0 + e1 + e2)


def _attn_combine(os_, ls_, tm):
    r, n = os_[0].shape
    spec = pl.BlockSpec((tm, n), lambda i: (i, 0))
    return pl.pallas_call(
        _attn_combine_kernel,
        grid=(r // tm,),
        in_specs=[spec] * 6,
        out_specs=spec,
        out_shape=jax.ShapeDtypeStruct((r, n), _F32),
        compiler_params=_cparams("parallel"),
        name="attn_combine",
    )(*os_, *ls_)


def _attn_dec_kernel(q_ref, kn_ref, vn_ref, c0_ref, c1_ref, c2_ref, o_ref, *, t_new):
    tp = q_ref.shape[1]
    outs, lses = [], []
    for gi, c_ref in enumerate((c0_ref, c1_ref, c2_ref)):
        d = C_DILATIONS[gi]
        L = c_ref.shape[1]
        q = q_ref[gi]
        s_c = _dot_nt(q, c_ref[0]) * C_SCALE
        s_n = _dot_nt(q, kn_ref[gi]) * C_SCALE
        qi = lax.broadcasted_iota(jnp.int32, (tp, L), 0)
        kj = lax.broadcasted_iota(jnp.int32, (tp, L), 1)
        dist = L + qi - kj
        ok_c = (dist % d == 0) & (dist <= (C_NKEYS - 1) * d)
        qn = lax.broadcasted_iota(jnp.int32, (tp, tp), 0)
        kn = lax.broadcasted_iota(jnp.int32, (tp, tp), 1)
        dn = qn - kn
        ok_n = (dn >= 0) & (dn % d == 0) & (dn <= (C_NKEYS - 1) * d) & (kn < t_new)
        s_c = jnp.where(ok_c, s_c, -jnp.inf)
        s_n = jnp.where(ok_n, s_n, -jnp.inf)
        m = jnp.maximum(jnp.max(s_c, axis=-1, keepdims=True), jnp.max(s_n, axis=-1, keepdims=True))
        p_c = jnp.exp(s_c - m)
        p_n = jnp.exp(s_n - m)
        l = jnp.sum(p_c, axis=-1, keepdims=True) + jnp.sum(p_n, axis=-1, keepdims=True)
        outs.append((_dot(p_c, c_ref[1]) + _dot(p_n, vn_ref[gi])) / l)
        lses.append(m + jnp.log(l))
    m = jnp.maximum(jnp.maximum(lses[0], lses[1]), lses[2])
    es = [jnp.exp(x - m) for x in lses]
    o_ref[...] = (es[0] * outs[0] + es[1] * outs[1] + es[2] * outs[2]) / (es[0] + es[1] + es[2])


def _attn_dec(qkv, caches, t_new):
    b, tp = qkv.shape[:2]
    x = jnp.transpose(qkv, (2, 0, 3, 1, 4))
    new = lambda s: pl.BlockSpec((None, None, C_GROUPS, tp, C_HEAD),
                                 lambda bi, hi: (s, bi, 0, 0, hi))
    cspec = lambda c: pl.BlockSpec((2, None, c.shape[2], C_HEAD), lambda bi, hi: (0, bi, 0, hi))
    return pl.pallas_call(
        functools.partial(_attn_dec_kernel, t_new=t_new),
        grid=(b, C_HEADS),
        in_specs=[new(0), new(1), new(2)] + [cspec(c) for c in caches],
        out_specs=pl.BlockSpec((None, tp, C_HEAD), lambda bi, hi: (bi, 0, hi)),
        out_shape=jax.ShapeDtypeStruct((b, tp, C_WIDTH), _F32),
        compiler_params=_cparams("parallel", "parallel"),
        name="attn_decode",
    )(x, x, x, *caches)


def _pad_rows(x, n, axis=1):
    pad = [(0, 0)] * x.ndim
    pad[axis] = (0, n - x.shape[axis])
    return jnp.pad(x, pad)


SEQ_TM = 512
SEQ_TM_PREP = 256
SEQ_TM_MIX = 128
SEQ_CHUNK = 64
DEC_CHUNK = 32
FFN_TF = 256


def _trunk(x, mods, shift_s, wkv_s, convb_s, ffn_s, kv_s, W, decode):
    bn, t, d = x.shape
    r = bn * t
    xf = x.reshape(r, d)
    if decode:
        tm = r
        bpb = 1
        modv = lambda l, k: jnp.repeat(mods[l, :, k], t, axis=0)[None]
    else:
        tm = min(SEQ_TM, t)
        bpb = t // tm
        modv = lambda l, k: mods[l, :, k][:, None, :]
    new_shift, new_wkv, new_convb, new_ffn = [], [], [], []
    new_kv = [[] for _ in C_WINDOWS]
    a3 = 3 * A_WIDTH
    for l in range(DEPTH):
        if l % 2 == 0:
            e = l // 2
            n_in = W['ab_w_in'].shape[2]
            p, h = _premix(xf, W['g_pre_mix'][l], modv(l, 0), modv(l, 1), W['ab_w_in'][e],
                           n_in, tm, 512, bpb)
            new_shift.append(h.reshape(bn, t, d)[:, -1])
            if decode:
                sp = shift_s[e]
                pa_st = _plain_mm(sp, W['ab_w_in'][e], a3, 512)
                place = lambda s: jnp.zeros((bn, t, s.shape[-1]), _F32).at[:, 0].set(s).reshape(r, -1)
                outs = _rwkv_prep(h, p, W, e, tm, bpb, first=(place(sp), place(pa_st)), seg=t)
            else:
                tmp = min(SEQ_TM_PREP, t)
                outs = _rwkv_prep(h, p, W, e, tmp, t // tmp,
                                  state=(jnp.zeros((bn, 1, d), _F32), jnp.zeros((bn, 1, a3), _F32)))
            rr, lw, k2, vv, av, bv, gg, bon, u = [o.reshape(bn, t, -1) for o in outs]
            if decode:
                chunk = DEC_CHUNK
                scan_in = [_pad_rows(a_, chunk) for a_ in (rr, k2, vv, lw, av, bv)]
                z0 = _state_to_bd(wkv_s[e].astype(_F32))
            else:
                chunk = SEQ_CHUNK
                scan_in = [rr, k2, vv, lw, av, bv]
                z0 = jnp.zeros((bn, SCAN_NG, SCAN_GL, SCAN_GL), _F32)
            o, zf = _wkv_scan(*scan_in, z0, chunk)
            new_wkv.append(_bd_to_state(zf))
            cprev = convb_s[e] if decode else jnp.zeros((bn, B_CONV - 1, B_WIDTH), _F32)
            xx = jnp.concatenate([cprev, u], axis=1)
            new_convb.append(xx[:, t:])
            if decode:
                tp = _CONV_HALO
                mo = _mix_out(_pad_rows(o[:, :t], tp), _pad_rows(bon, tp), _pad_rows(gg, tp),
                              _pad_rows(xx, tp + _CONV_HALO), W, e, tp)[:, :t]
            else:
                mo = _mix_out(o, bon, gg, _pad_rows(xx, t + _CONV_HALO), W, e, SEQ_TM_MIX)
            xf = _post_mm(mo.reshape(r, -1), W['ab_w_out'][e], xf, W['g_post_mix'][l], modv(l, 2),
                          tm, 512, bpb)
        else:
            oi = l // 2
            n_qkv = W['attn_w_qkv'].shape[2]
            qkv, _ = _premix(xf, W['g_pre_mix'][l], modv(l, 0), modv(l, 1), W['attn_w_qkv'][oi],
                             n_qkv, tm, 512, bpb)
            q5 = qkv.reshape(bn, t, 3, C_GROUPS, C_HEADS, C_HEAD)
            for gi in range(C_GROUPS):
                keep = min(C_WINDOWS[gi], t)
                new_kv[gi].append(jnp.stack([q5[:, t - keep:, 1, gi], q5[:, t - keep:, 2, gi]]))
            if decode:
                tp = LANES
                qp = _pad_rows(qkv.reshape(bn, t, 3, C_GROUPS, C_WIDTH), tp)
                caches = [kv[oi].reshape(2, bn, kv.shape[3], C_WIDTH) for kv in kv_s]
                att = _attn_dec(qp, caches, t)[:, :t].reshape(r, C_WIDTH)
            else:
                q3 = qkv.reshape(bn, t, n_qkv)
                res = [_attn_seq(q3, gi, 128) for gi in range(C_GROUPS)]
                att = _attn_combine([o_.reshape(r, C_WIDTH) for o_, _ in res],
                                    [l_.reshape(r, C_WIDTH) for _, l_ in res], 512)
            xf = _post_mm(att, W['attn_w_out'][oi], xf, W['g_post_mix'][l], modv(l, 2), tm, 512, bpb)
        if decode:
            st = ffn_s[l]
            f = st.shape[-1]
            p1 = jnp.zeros((bn, t, f), _F32).at[:, 0].set(st[:, 1]).reshape(r, f)
            p2 = jnp.zeros((bn, t, f), _F32).at[:, 0].set(st[:, 0]).at[:, 1].set(st[:, 1]).reshape(r, f)
            xf, gt = _ffn(xf, W['g_pre_ffn'][l], modv(l, 3), modv(l, 4), W['ffn_w_gate'][l],
                          W['ffn_w_up'][l], W['ffn_conv_w'][l], W['ffn_conv_b'][l], W['ffn_w_down'][l],
                          W['g_post_ffn'][l], modv(l, 5), tm, FFN_TF, bpb, prev=(p1, p2), seg=t)
            new_ffn.append(gt.reshape(bn, t, f)[:, t - 2:])
        else:
            xf, gt = _ffn(xf, W['g_pre_ffn'][l], modv(l, 3), modv(l, 4), W['ffn_w_gate'][l],
                          W['ffn_w_up'][l], W['ffn_conv_w'][l], W['ffn_conv_b'][l], W['ffn_w_down'][l],
                          W['g_post_ffn'][l], modv(l, 5), tm, FFN_TF, bpb)
            f = gt.shape[-1]
            new_ffn.append(gt.reshape(bn, bpb, SUBLANES, f)[:, -1, SUBLANES - 2:])
    return (xf.reshape(bn, t, d), jnp.stack(new_shift), jnp.stack(new_wkv), jnp.stack(new_convb),
            jnp.stack(new_ffn), jnp.stack(new_kv[0]), jnp.stack(new_kv[1]), jnp.stack(new_kv[2]))


def kernel(x_prompt, x_sample, state_shift, state_wkv, state_conv_b, state_ffn, cache_kv_w128, cache_kv_w512, cache_kv_w2048, c_prompt, c_sample, w_mod, b_mod, g_pre_mix, g_post_mix, g_pre_ffn, g_post_ffn, ab_w_in, a_mu_rkv, a_mu_wag, a_w0, a_w1, a_w2, a_a0, a_a1, a_a2, a_g1, a_g2, a_k_k, a_k_a, a_r_k, a_ln_w, a_ln_b, b_conv_w, b_conv_b, b_ln_w, b_ln_b, ab_w_out, attn_w_qkv, attn_w_out, ffn_w_gate, ffn_w_up, ffn_conv_w, ffn_conv_b, ffn_w_down):
    W = dict(w_mod=w_mod, b_mod=b_mod, g_pre_mix=g_pre_mix, g_post_mix=g_post_mix,
             g_pre_ffn=g_pre_ffn, g_post_ffn=g_post_ffn, ab_w_in=ab_w_in, a_mu_rkv=a_mu_rkv,
             a_mu_wag=a_mu_wag, a_w0=a_w0, a_w1=a_w1, a_w2=a_w2, a_a0=a_a0, a_a1=a_a1, a_a2=a_a2,
             a_g1=a_g1, a_g2=a_g2, a_k_k=a_k_k, a_k_a=a_k_a, a_r_k=a_r_k, a_ln_w=a_ln_w, a_ln_b=a_ln_b,
             b_conv_w=b_conv_w, b_conv_b=b_conv_b, b_ln_w=b_ln_w, b_ln_b=b_ln_b, ab_w_out=ab_w_out,
             attn_w_qkv=attn_w_qkv, attn_w_out=attn_w_out, ffn_w_gate=ffn_w_gate, ffn_w_up=ffn_w_up,
             ffn_conv_w=ffn_conv_w, ffn_conv_b=ffn_conv_b, ffn_w_down=ffn_w_down)
    nbp = x_prompt.shape[0]
    nbs = x_sample.shape[0]
    d = x_prompt.shape[-1]
    assert d == D_MODEL and w_mod.shape == (DEPTH, d, N_MOD * d)
    c_all = _pad_rows(jnp.concatenate([c_prompt, c_sample], axis=0), 16, axis=0)
    mods = _mods(c_all, w_mod, b_mod).reshape(DEPTH, 16, N_MOD, d)
    outs_p = _trunk(x_prompt, mods[:, :nbp], None, None, None, None, None, W, decode=False)
    outs_s = _trunk(x_sample, mods[:, nbp:nbp + nbs], state_shift, state_wkv, state_conv_b,
                    state_ffn, (cache_kv_w128, cache_kv_w512, cache_kv_w2048), W, decode=True)
    (y_p, p_shift, p_wkv, p_conv_b, p_ffn, p_kv0, p_kv1, p_kv2) = outs_p
    (y_s, s_shift, s_wkv, s_conv_b, s_ffn, s_kv0, s_kv1, s_kv2) = outs_s
    return (y_p, y_s, p_shift, p_wkv, p_conv_b, p_ffn, p_kv0, p_kv1, p_kv2,
            s_shift, s_wkv, s_conv_b, s_ffn, s_kv0, s_kv1, s_kv2)
```

```python
import functools

import jax
import jax.numpy as jnp
from jax import lax
from jax.experimental import pallas as pl
from jax.experimental.pallas import tpu as pltpu

_BF = jnp.bfloat16
_F32 = jnp.float32

D_MODEL = 2048
DEPTH = 4
N_MOD = 6
A_HEAD = 64
A_HEADS = 16
A_WIDTH = A_HEADS * A_HEAD
A_GN_EPS = 64e-5
B_WIDTH = 1024
B_CONV = 31
C_WINDOWS = (128, 512, 2048)
C_DILATIONS = (1, 4, 16)
C_GROUPS = 3
C_HEADS = 8
C_HEAD = 128
C_WIDTH = C_HEADS * C_HEAD
C_SCALE = C_HEAD ** -0.5
C_NKEYS = 129
D_FF = 5632
RMS_EPS = 1e-6
LN_EPS = 1e-5

SUBLANES = 8
LANES = 128
VMEM_LIMIT = 56 * 1024 * 1024

SCAN_G = 4
SCAN_GL = SCAN_G * A_HEAD
SCAN_NG = A_HEADS // SCAN_G


def _cparams(*sem):
    return pltpu.CompilerParams(dimension_semantics=sem, vmem_limit_bytes=VMEM_LIMIT)


def _dot(a, b):
    return jnp.dot(a.astype(_BF), b.astype(_BF), preferred_element_type=_F32)


def _dot_nt(a, b):
    return lax.dot_general(a.astype(_BF), b.astype(_BF), (((1,), (1,)), ((), ())),
                           preferred_element_type=_F32)


def _dot_tn(a, b):
    return lax.dot_general(a.astype(_BF), b.astype(_BF), (((0,), (0,)), ((), ())),
                           preferred_element_type=_F32)


def _split3(x):
    h1 = x.astype(_BF)
    r1 = x - h1.astype(_F32)
    h2 = r1.astype(_BF)
    h3 = (r1 - h2.astype(_F32)).astype(_BF)
    return h1, h2, h3


def _dot_exact_rhs(x, m_bf):
    h1, h2, h3 = _split3(x)
    return (jnp.dot(h1, m_bf, preferred_element_type=_F32)
            + jnp.dot(h2, m_bf, preferred_element_type=_F32)
            + jnp.dot(h3, m_bf, preferred_element_type=_F32))


def _split2(x):
    hi = x.astype(_BF)
    return hi, (x - hi.astype(_F32)).astype(_BF)


def _dot_hilo(lhs_list, rhs):
    his, los = zip(*[_split2(a) for a in lhs_list])
    rh, rl = _split2(rhs)
    y1 = jnp.dot(jnp.concatenate(his + los, axis=0), rh, preferred_element_type=_F32)
    y2 = jnp.dot(jnp.concatenate(his, axis=0), rl, preferred_element_type=_F32)
    tot = sum(a.shape[0] for a in lhs_list)
    outs, off = [], 0
    for a in lhs_list:
        m = a.shape[0]
        outs.append(y1[off:off + m] + y1[tot + off:tot + off + m] + y2[off:off + m])
        off += m
    return outs


def _sigmoid(x):
    return 1.0 / (1.0 + jnp.exp(-x))


def _silu(x):
    return x * _sigmoid(x)


def _rms(y, g):
    return y * lax.rsqrt(jnp.mean(y * y, axis=-1, keepdims=True) + RMS_EPS) * g


def _mod_kernel(c_ref, w_ref, b_ref, o_ref):
    c = c_ref[...]
    ch = c.astype(_BF)
    cl = (c - ch.astype(_F32)).astype(_BF)
    y = jnp.dot(jnp.concatenate([ch, cl], axis=0), w_ref[...].astype(_BF),
                preferred_element_type=_F32)
    n = c.shape[0]
    o_ref[...] = y[:n] + y[n:] + b_ref[...]


def _mods(c_all, w_mod, b_mod):
    depth, d, n = w_mod.shape
    rows = c_all.shape[0]
    tn = 1024
    return pl.pallas_call(
        _mod_kernel,
        grid=(depth, n // tn),
        in_specs=[pl.BlockSpec((rows, d), lambda l, j: (0, 0)),
                  pl.BlockSpec((None, d, tn), lambda l, j: (l, 0, j)),
                  pl.BlockSpec((None, 1, tn), lambda l, j: (l, 0, j))],
        out_specs=pl.BlockSpec((None, rows, tn), lambda l, j: (l, 0, j)),
        out_shape=jax.ShapeDtypeStruct((depth, rows, n), _F32),
        compiler_params=_cparams("parallel", "parallel"),
        name="mods",
    )(c_all, w_mod, b_mod.reshape(depth, 1, n))


def _premix_kernel(x_ref, g_ref, sh_ref, sc_ref, w_ref, o_ref, *rest):
    hs_ref = rest[-1]

    @pl.when(pl.program_id(1) == 0)
    def _():
        h = _rms(x_ref[...], g_ref[...]) * (1.0 + sc_ref[...]) + sh_ref[...]
        if len(rest) == 2:
            rest[0][...] = h
        hs_ref[...] = h.astype(_BF)

    o_ref[...] = jnp.dot(hs_ref[...], w_ref[...].astype(_BF), preferred_element_type=_F32)


def _mod_spec(mod, tm, bpb):
    ms = mod.shape[1]
    d = mod.shape[2]
    if ms == 1:
        return pl.BlockSpec((None, 1, d), lambda i, *_: (i // bpb, 0, 0))
    assert mod.shape[0] == 1 and ms == tm
    return pl.BlockSpec((None, ms, d), lambda i, *_: (0, 0, 0))


def _once(block_shape, index_map):
    return pl.BlockSpec(block_shape, index_map, pipeline_mode=pl.Buffered(1))


def _premix(x, g, shift, scale, w, n_out, tm, tn, bpb, emit_h):
    r, d = x.shape
    out_specs = [pl.BlockSpec((tm, tn), lambda i, j: (i, j))]
    out_shape = [jax.ShapeDtypeStruct((r, n_out), _F32)]
    if emit_h:
        out_specs.append(pl.BlockSpec((tm, d), lambda i, j: (i, 0)))
        out_shape.append(jax.ShapeDtypeStruct((r, d), _F32))
    return pl.pallas_call(
        _premix_kernel,
        grid=(r // tm, n_out // tn),
        in_specs=[_once((tm, d), lambda i, j: (i, 0)),
                  pl.BlockSpec((1, d), lambda i, j: (0, 0)),
                  _mod_spec(shift, tm, bpb), _mod_spec(scale, tm, bpb),
                  pl.BlockSpec((d, tn), lambda i, j: (0, j))],
        out_specs=out_specs,
        out_shape=out_shape,
        scratch_shapes=[pltpu.VMEM((tm, d), _BF)],
        compiler_params=_cparams("parallel", "arbitrary"),
        name="premix_matmul",
    )(x, g.reshape(1, d), shift, scale, w)


def _plain_mm_kernel(a_ref, w_ref, o_ref):
    o_ref[...] = _dot(a_ref[...], w_ref[...])


def _plain_mm(a, w, n_out, tn):
    m, k = a.shape
    return pl.pallas_call(
        _plain_mm_kernel,
        grid=(n_out // tn,),
        in_specs=[pl.BlockSpec((m, k), lambda j: (0, 0)),
                  pl.BlockSpec((k, tn), lambda j: (0, j))],
        out_specs=pl.BlockSpec((m, tn), lambda j: (0, j)),
        out_shape=jax.ShapeDtypeStruct((m, n_out), _F32),
        compiler_params=_cparams("parallel"),
        name="plain_matmul",
    )(a, w)


def _post_kernel(a_ref, w_ref, x_ref, g_ref, gm_ref, o_ref):
    k = pl.program_id(1)
    y = _dot(a_ref[...], w_ref[...])

    @pl.when(k == 0)
    def _():
        o_ref[...] = y

    @pl.when(k > 0)
    def _():
        o_ref[...] += y

    @pl.when(k == pl.num_programs(1) - 1)
    def _():
        o_ref[...] = x_ref[...] + gm_ref[...] * _rms(o_ref[...], g_ref[...])


def _post_mm(a, w, x, g, gate, tm, tk, bpb):
    r, kdim = a.shape
    d = w.shape[1]
    return pl.pallas_call(
        _post_kernel,
        grid=(r // tm, kdim // tk),
        in_specs=[pl.BlockSpec((tm, tk), lambda i, k: (i, k)),
                  pl.BlockSpec((tk, d), lambda i, k: (k, 0)),
                  _once((tm, d), lambda i, k: (i, 0)),
                  pl.BlockSpec((1, d), lambda i, k: (0, 0)),
                  _mod_spec(gate, tm, bpb)],
        out_specs=pl.BlockSpec((tm, d), lambda i, k: (i, 0)),
        out_shape=jax.ShapeDtypeStruct((r, d), _F32),
        compiler_params=_cparams("parallel", "arbitrary"),
        name="post_matmul",
    )(a, w, x, g.reshape(1, d), gate)


def _ffn_kernel(*refs, seq_mode, bpb, seg):
    if seq_mode:
        (x_ref, gpre_ref, sh_ref, sc_ref, wg_ref, wu_ref, cw_ref, cb_ref, wd_ref, gpost_ref,
         gm_ref, o_ref, gt_ref, hs_ref, carry_ref) = refs
    else:
        (x_ref, gpre_ref, sh_ref, sc_ref, wg_ref, wu_ref, cw_ref, cb_ref, wd_ref, gpost_ref,
         gm_ref, p1_ref, p2_ref, o_ref, gt_ref, hs_ref) = refs
    i = pl.program_id(0)
    j = pl.program_id(1)

    @pl.when(j == 0)
    def _():
        h = _rms(x_ref[...], gpre_ref[...]) * (1.0 + sc_ref[...]) + sh_ref[...]
        hs_ref[...] = h.astype(_BF)

    hs = hs_ref[...]
    gate = jnp.dot(hs, wg_ref[...].astype(_BF), preferred_element_type=_F32)
    up = jnp.dot(hs, wu_ref[...].astype(_BF), preferred_element_type=_F32)
    tm = gate.shape[0]
    rows = lax.broadcasted_iota(jnp.int32, gate.shape, 0)
    g1 = pltpu.roll(gate, 1, axis=0)
    g2 = pltpu.roll(gate, 2, axis=0)
    if seq_mode:
        @pl.when(i % bpb == 0)
        def _():
            carry_ref[j] = jnp.zeros((SUBLANES, gate.shape[1]), _F32)

        tail = carry_ref[j]
        m1, m2 = tail[SUBLANES - 1:SUBLANES], tail[SUBLANES - 2:SUBLANES - 1]
        g1 = jnp.where(rows == 0, m1, g1)
        g2 = jnp.where(rows == 0, m2, jnp.where(rows == 1, m1, g2))
        carry_ref[j] = gate[tm - SUBLANES:]
        gt_ref[...] = gate[tm - SUBLANES:]
    else:
        rs = rows % seg
        g1 = jnp.where(rs == 0, p1_ref[...], g1)
        g2 = jnp.where(rs < 2, p2_ref[...], g2)
        gt_ref[...] = gate
    cw = cw_ref[...]
    gc = cw[0:1] * g2 + cw[1:2] * g1 + cw[2:3] * gate + cb_ref[...]
    act = _silu(gc) * up
    y = jnp.dot(act.astype(_BF), wd_ref[...].astype(_BF), preferred_element_type=_F32)

    @pl.when(j == 0)
    def _():
        o_ref[...] = y

    @pl.when(j > 0)
    def _():
        o_ref[...] += y

    @pl.when(j == pl.num_programs(1) - 1)
    def _():
        o_ref[...] = x_ref[...] + gm_ref[...] * _rms(o_ref[...], gpost_ref[...])


def _ffn(x, gpre, shift, scale, wg, wu, cw, cb, wd, gpost, gate, tm, tf, bpb, prev=None, seg=1):
    r, d = x.shape
    f = wg.shape[1]
    nblk = r // tm
    seq_mode = prev is None
    gt_rows = SUBLANES if seq_mode else tm
    in_specs = [_once((tm, d), lambda i, j: (i, 0)),
                pl.BlockSpec((1, d), lambda i, j: (0, 0)),
                _mod_spec(shift, tm, bpb), _mod_spec(scale, tm, bpb),
                pl.BlockSpec((d, tf), lambda i, j: (0, j)),
                pl.BlockSpec((d, tf), lambda i, j: (0, j)),
                pl.BlockSpec((cw.shape[0], tf), lambda i, j: (0, j)),
                pl.BlockSpec((1, tf), lambda i, j: (0, j)),
                pl.BlockSpec((tf, d), lambda i, j: (j, 0)),
                pl.BlockSpec((1, d), lambda i, j: (0, 0)),
                _mod_spec(gate, tm, bpb)]
    args = [x, gpre.reshape(1, d), shift, scale, wg, wu, cw, cb.reshape(1, f), wd,
            gpost.reshape(1, d), gate]
    scratch = [pltpu.VMEM((tm, d), _BF)]
    if seq_mode:
        scratch.append(pltpu.VMEM((f // tf, SUBLANES, tf), _F32))
    else:
        in_specs += [pl.BlockSpec((tm, tf), lambda i, j: (i, j))] * 2
        args += list(prev)
    return pl.pallas_call(
        functools.partial(_ffn_kernel, seq_mode=seq_mode, bpb=bpb, seg=seg),
        grid=(nblk, f // tf),
        in_specs=in_specs,
        out_specs=[_once((tm, d), lambda i, j: (i, 0)),
                   pl.BlockSpec((None, gt_rows, tf), lambda i, j: (i, 0, j))],
        out_shape=[jax.ShapeDtypeStruct((r, d), _F32),
                   jax.ShapeDtypeStruct((nblk, gt_rows, f), _F32)],
        scratch_shapes=scratch,
        compiler_params=_cparams("arbitrary", "arbitrary"),
        name="conv_ffn",
    )(*args)


def _prev_rows(cur, first_row_of):
    return first_row_of(pltpu.roll(cur, 1, axis=0))


def _rwkv_prep_kernel(*refs, seq_mode, bpb, seg):
    if seq_mode:
        (h_ref, h8_ref, hst_ref, pa_ref, pa8_ref, past_ref, pb1_ref, pb2_ref) = refs[:8]
        rest = refs[8:]
    else:
        (h_ref, hf_ref, pa_ref, paf_ref, pb1_ref, pb2_ref) = refs[:6]
        rest = refs[6:]
    (murkv_ref, muwag_ref, w0_ref, w1_ref, w2_ref, a0_ref, a1_ref, a2_ref, g1_ref, g2_ref,
     kk_ref, ka_ref, rk_ref, ones_ref,
     r_out, lw_out, k_out, v_out, av_out, bv_out, g_out, bon_out, u_out) = rest
    i = pl.program_id(0)
    h = h_ref[...]
    pa = pa_ref[...]
    rows_h = lax.broadcasted_iota(jnp.int32, h.shape, 0)
    rows_p = lax.broadcasted_iota(jnp.int32, pa.shape, 0)
    hp = pltpu.roll(h, 1, axis=0)
    pp = pltpu.roll(pa, 1, axis=0)
    if seq_mode:
        first = i % bpb == 0
        h0 = jnp.where(first, hst_ref[...], h8_ref[SUBLANES - 1:SUBLANES])
        p0 = jnp.where(first, past_ref[...], pa8_ref[SUBLANES - 1:SUBLANES])
        hp = jnp.where(rows_h == 0, h0, hp)
        pp = jnp.where(rows_p == 0, p0, pp)
    else:
        hp = jnp.where(rows_h % seg == 0, hf_ref[...], hp)
        pp = jnp.where(rows_p % seg == 0, paf_ref[...], pp)
    delta = hp - h
    mu = muwag_ref[...]
    xw = h + delta * mu[0:1]
    xa = h + delta * mu[1:2]
    xg = h + delta * mu[2:3]
    zw = w0_ref[...] + _dot(jnp.tanh(_dot(xw, w1_ref[...])), w2_ref[...])
    w_log = -(jnp.maximum(-zw, 0.0) + jnp.log(1.0 + jnp.exp(-jnp.abs(zw)))) - 0.5
    lw_out[...] = -jnp.exp(w_log)
    a = _sigmoid(a0_ref[...] + _dot(_dot(xa, a1_ref[...]), a2_ref[...]))
    g_out[...] = _dot(_sigmoid(_dot(xg, g1_ref[...])), g2_ref[...])
    rkv = pa + (pp - pa) * murkv_ref[...]
    r = rkv[:, :A_WIDTH]
    k = rkv[:, A_WIDTH:2 * A_WIDTH]
    v = rkv[:, 2 * A_WIDTH:]
    ones = ones_ref[...]

    def segsum(x):
        parts = [_dot_exact_rhs(x[:, c * SCAN_GL:(c + 1) * SCAN_GL], ones)
                 for c in range(A_WIDTH // SCAN_GL)]
        return jnp.concatenate(parts, axis=1)

    kk = k * kk_ref[...]
    kk = kk * lax.rsqrt(jnp.maximum(segsum(kk * kk), 1e-24))
    k2 = k * (1.0 + (a - 1.0) * ka_ref[...])
    r_out[...] = r
    k_out[...] = k2
    v_out[...] = v
    av_out[...] = -kk
    bv_out[...] = kk * a
    bon_out[...] = segsum(r * k2 * rk_ref[...]) * v
    u_out[...] = pb1_ref[...] * _sigmoid(pb2_ref[...])


def _head_ones():
    idx = jnp.arange(SCAN_GL) // A_HEAD
    return (idx[:, None] == idx[None, :]).astype(_BF)


def _rwkv_prep(h, p, W, e, tm, bpb, state=None, first=None, seg=1):
    r, d = h.shape
    a3 = 3 * A_WIDTH
    seq_mode = first is None
    row = lambda n: pl.BlockSpec((tm, n), lambda i: (i, 0))
    full = lambda s: pl.BlockSpec(s, lambda i: (0,) * len(s))
    if seq_mode:
        t8 = tm // SUBLANES
        prev8 = lambda n: pl.BlockSpec((SUBLANES, n), lambda i: (jnp.maximum(i * t8 - 1, 0), 0))
        in_specs = [row(d), prev8(d), pl.BlockSpec((None, 1, d), lambda i: (i // bpb, 0, 0)),
                    row(a3), prev8(a3), pl.BlockSpec((None, 1, a3), lambda i: (i // bpb, 0, 0))]
        args = [h, h, state[0], p, p, state[1]]
    else:
        in_specs = [row(d), row(d), row(a3), row(a3)]
        args = [h, first[0], p, first[1]]
    in_specs += [pl.BlockSpec((tm, B_WIDTH), lambda i: (i, a3 // B_WIDTH)),
                 pl.BlockSpec((tm, B_WIDTH), lambda i: (i, a3 // B_WIDTH + 1))]
    args += [p, p]
    vec = lambda x: x.reshape(1, -1)
    lora = lambda n: -(-n // LANES) * LANES
    pad_c = lambda x: _pad_rows(x, lora(x.shape[1]), axis=1)
    pad_r = lambda x: _pad_rows(x, lora(x.shape[0]), axis=0)
    params = [vec(W['a_mu_rkv'][e]), W['a_mu_wag'][e], vec(W['a_w0'][e]), pad_c(W['a_w1'][e]),
              pad_r(W['a_w2'][e]), vec(W['a_a0'][e]), pad_c(W['a_a1'][e]), pad_r(W['a_a2'][e]),
              W['a_g1'][e],
              W['a_g2'][e], vec(W['a_k_k'][e]), vec(W['a_k_a'][e]), vec(W['a_r_k'][e]),
              _head_ones()]
    in_specs += [full(x.shape) for x in params]
    args += params
    out = jax.ShapeDtypeStruct((r, A_WIDTH), _F32)
    return pl.pallas_call(
        functools.partial(_rwkv_prep_kernel, seq_mode=seq_mode, bpb=bpb, seg=seg),
        grid=(r // tm,),
        in_specs=in_specs,
        out_specs=[row(A_WIDTH)] * 9,
        out_shape=[out] * 9,
        compiler_params=_cparams("parallel"),
        name="rwkv_prep",
    )(*args)


def _scan_groups(rcs, kcs, vcs, lws, avs, bvs, zbds, ltri):
    n = len(rcs)
    ix = range(n)
    c = rcs[0].shape[0]
    gc = SCAN_G * c
    cum = [_dot_exact_rhs_left(ltri, lws[i]) for i in ix]
    cl = [cum[i][c - 1:c] for i in ix]
    e_dn = [jnp.exp(-cum[i]) for i in ix]
    e_cl = [jnp.exp(cl[i] - cum[i]) for i in ix]
    rt = [rcs[i] * jnp.exp(cum[i]) for i in ix]
    at = [avs[i] * jnp.exp(cum[i] - lws[i]) for i in ix]
    kt = [kcs[i] * e_dn[i] for i in ix]
    bt = [bvs[i] * e_dn[i] for i in ix]
    kh = [kcs[i] * e_cl[i] for i in ix]
    bh = [bvs[i] * e_cl[i] for i in ix]

    row_h = lax.broadcasted_iota(jnp.int32, (gc, SCAN_GL), 0) // c
    lane_h = lax.broadcasted_iota(jnp.int32, (gc, SCAN_GL), 1) // A_HEAD
    hm_e = row_h == lane_h

    def expand(x):
        return jnp.where(hm_e, jnp.concatenate([x] * SCAN_G, axis=0), 0.0)

    bd_cc = (lax.broadcasted_iota(jnp.int32, (gc, gc), 0) // c
             == lax.broadcasted_iota(jnp.int32, (gc, gc), 1) // c)

    def bdiag(x):
        return jnp.where(bd_cc, jnp.concatenate([x] * SCAN_G, axis=0), 0.0)

    t_idx = lax.broadcasted_iota(jnp.int32, (c, gc), 0)
    j_idx = lax.broadcasted_iota(jnp.int32, (c, gc), 1) % c
    lo_s = j_idx < t_idx
    lo_i = j_idx <= t_idx
    amat = [_dot_nt(jnp.concatenate([at[i], rt[i]], axis=0),
                    jnp.concatenate([expand(kt[i]), expand(bt[i])], axis=0)) for i in ix]
    a_ak = [jnp.where(lo_s, amat[i][:c, :gc], 0.0) for i in ix]
    a_ab = [jnp.where(lo_s, amat[i][:c, gc:], 0.0) for i in ix]
    a_rk = [jnp.where(lo_i, amat[i][c:, :gc], 0.0) for i in ix]
    a_rb = [jnp.where(lo_i, amat[i][c:, gc:], 0.0) for i in ix]
    eye = jnp.where(j_idx == t_idx, 1.0, 0.0)
    trow = [eye + a_ab[i] for i in ix]
    pw = list(a_ab)
    nlev = c.bit_length() - 1
    for lev in range(nlev):
        for i in ix:
            lhs = ([trow[i]] if lev >= 1 else []) + ([pw[i]] if lev < nlev - 1 else [])
            res = _dot_hilo(lhs, bdiag(pw[i]))
            if lev >= 1:
                trow[i] = trow[i] + res[0]
            if lev < nlev - 1:
                pw[i] = res[-1]
    ve = [expand(vcs[i]) for i in ix]
    x1 = [_dot(a_ak[i], ve[i]) for i in ix]
    tu = [_dot(trow[i], jnp.concatenate([expand(x1[i]), expand(at[i])], axis=1)) for i in ix]
    u0 = [tu[i][:, :SCAN_GL] for i in ix]
    ap = [tu[i][:, SCAN_GL:] for i in ix]
    ru = [_dot(a_rb[i], jnp.concatenate([expand(ap[i]), expand(u0[i])], axis=1)) for i in ix]
    rp = [rt[i] + ru[i][:, :SCAN_GL] for i in ix]
    o0 = [_dot(a_rk[i], ve[i]) + ru[i][:, SCAN_GL:] for i in ix]
    sz = [_dot(jnp.concatenate([rp[i], ap[i]], axis=0), zbds[i]) for i in ix]
    o = [sz[i][:c] + o0[i] for i in ix]
    u = [sz[i][c:] + u0[i] for i in ix]
    dmask = (lax.broadcasted_iota(jnp.int32, (A_HEAD, SCAN_GL), 1) % A_HEAD
             == lax.broadcasted_iota(jnp.int32, (A_HEAD, SCAN_GL), 0))
    bd_ll = (lax.broadcasted_iota(jnp.int32, (SCAN_GL, SCAN_GL), 0) // A_HEAD
             == lax.broadcasted_iota(jnp.int32, (SCAN_GL, SCAN_GL), 1) // A_HEAD)
    znew = []
    for i in ix:
        zc = zbds[i][0:A_HEAD]
        for h in range(1, SCAN_G):
            zc = zc + zbds[i][h * A_HEAD:(h + 1) * A_HEAD]
        dg = jnp.where(dmask, jnp.exp(cl[i]), 0.0)
        xs = jnp.concatenate([kh[i], bh[i], dg], axis=0)
        ys = jnp.concatenate([vcs[i], u[i], zc], axis=0)
        znew.append(jnp.where(bd_ll, _dot_tn(xs, ys), 0.0))
    return o, znew


def _dot_exact_rhs_left(m_bf, x):
    h1, h2, h3 = _split3(x)
    return (jnp.dot(m_bf, h1, preferred_element_type=_F32)
            + jnp.dot(m_bf, h2, preferred_element_type=_F32)
            + jnp.dot(m_bf, h3, preferred_element_type=_F32))


def _scan_kernel(r_ref, k_ref, v_ref, lw_ref, av_ref, bv_ref, z0_ref, o_ref, zout_ref, z_ref):
    t = pl.program_id(1)

    @pl.when(t == 0)
    def _():
        z_ref[...] = z0_ref[...]

    bs, c = r_ref.shape[:2]
    ltri = (lax.broadcasted_iota(jnp.int32, (c, c), 0)
            >= lax.broadcasted_iota(jnp.int32, (c, c), 1)).astype(_BF)
    chains = [(bi, gi) for bi in range(bs) for gi in range(SCAN_NG)]
    lanes = lambda gi: slice(gi * SCAN_GL, (gi + 1) * SCAN_GL)
    tok = lambda ref: [ref[bi, :, lanes(gi)] for bi, gi in chains]
    o, znew = _scan_groups(tok(r_ref), tok(k_ref), tok(v_ref), tok(lw_ref), tok(av_ref),
                           tok(bv_ref), [z_ref[bi, gi] for bi, gi in chains], ltri)
    for n, (bi, gi) in enumerate(chains):
        o_ref[bi, :, lanes(gi)] = o[n]
        z_ref[bi, gi] = znew[n]

    @pl.when(t == pl.num_programs(1) - 1)
    def _():
        zout_ref[...] = z_ref[...]


def _wkv_scan(r, k, v, lw, av, bv, z0, chunk, bs):
    b, t, _ = r.shape
    tok = pl.BlockSpec((bs, chunk, A_WIDTH), lambda bi, ti: (bi, ti, 0))
    zspec = pl.BlockSpec((bs, SCAN_NG, SCAN_GL, SCAN_GL), lambda bi, ti: (bi, 0, 0, 0))
    return pl.pallas_call(
        _scan_kernel,
        grid=(b // bs, t // chunk),
        in_specs=[tok] * 6 + [zspec],
        out_specs=[tok, zspec],
        out_shape=[jax.ShapeDtypeStruct((b, t, A_WIDTH), _F32),
                   jax.ShapeDtypeStruct((b, SCAN_NG, SCAN_GL, SCAN_GL), _F32)],
        scratch_shapes=[pltpu.VMEM((bs, SCAN_NG, SCAN_GL, SCAN_GL), _F32)],
        compiler_params=_cparams("parallel", "arbitrary"),
        name="wkv7_scan",
    )(r, k, v, lw, av, bv, z0)


def _state_to_bd(s):
    b = s.shape[0]
    st = jnp.swapaxes(s, -1, -2).reshape(b, SCAN_NG, SCAN_G, A_HEAD, A_HEAD)
    eye = jnp.eye(SCAN_G, dtype=s.dtype)
    z = st[:, :, :, :, None, :] * eye[None, None, :, None, :, None]
    return z.reshape(b, SCAN_NG, SCAN_GL, SCAN_GL)


def _bd_to_state(z):
    b = z.shape[0]
    z = z.reshape(b, SCAN_NG, SCAN_G, A_HEAD, SCAN_G, A_HEAD)
    idx = jnp.arange(SCAN_G)
    st = z[:, :, idx, :, idx, :]
    st = jnp.moveaxis(st, 0, 2).reshape(b, A_HEADS, A_HEAD, A_HEAD)
    return jnp.swapaxes(st, -1, -2)


def _mix_out_kernel(o_ref, bon_ref, g_ref, lnw_ref, lnb_ref, xm_ref, xh_ref, cw_ref, cb_ref,
                    blw_ref, blb_ref, ones_ref, out_ref, win_ref):
    tm = o_ref.shape[0]
    ones = ones_ref[...]

    def segmean(x):
        parts = [_dot_exact_rhs(x[:, c * SCAN_GL:(c + 1) * SCAN_GL], ones)
                 for c in range(A_WIDTH // SCAN_GL)]
        return jnp.concatenate(parts, axis=1) * (1.0 / A_HEAD)

    o = o_ref[...]
    oc = o - segmean(o)
    var = segmean(oc * oc)
    on = oc * lax.rsqrt(var + A_GN_EPS) * lnw_ref[...] + lnb_ref[...]
    out_ref[:, :A_WIDTH] = (on + bon_ref[...]) * g_ref[...]

    win_ref[:tm] = xm_ref[...]
    win_ref[tm:] = xh_ref[...]
    cols = []
    for c in range(B_WIDTH // LANES):
        cs = slice(c * LANES, (c + 1) * LANES)
        acc = jnp.zeros((tm, LANES), _F32) + cb_ref[:, cs]
        for j in range(B_CONV):
            acc = acc + cw_ref[j:j + 1, cs] * win_ref[j:j + tm, cs]
        cols.append(acc)
    ub = jnp.concatenate(cols, axis=1)
    uc = ub - jnp.mean(ub, axis=-1, keepdims=True)
    uv = jnp.mean(uc * uc, axis=-1, keepdims=True)
    out_ref[:, A_WIDTH:] = _silu(uc * lax.rsqrt(uv + LN_EPS) * blw_ref[...] + blb_ref[...])


_CONV_HALO = 32


def _mix_out(o, bon, g, xx, W, e, tm):
    b, t, _ = o.shape
    tok = pl.BlockSpec((None, tm, A_WIDTH), lambda bi, ti: (bi, ti, 0))
    full = lambda s: pl.BlockSpec(s, lambda bi, ti: (0,) * len(s))
    hb = tm // _CONV_HALO
    vec = lambda x: x.reshape(1, -1)
    params = [vec(W['a_ln_w'][e]), vec(W['a_ln_b'][e])]
    conv = [W['b_conv_w'][e], vec(W['b_conv_b'][e]), vec(W['b_ln_w'][e]), vec(W['b_ln_b'][e]),
            _head_ones()]
    return pl.pallas_call(
        _mix_out_kernel,
        grid=(b, t // tm),
        in_specs=[tok, tok, tok] + [full(x.shape) for x in params]
        + [pl.BlockSpec((None, tm, B_WIDTH), lambda bi, ti: (bi, ti, 0)),
           pl.BlockSpec((None, _CONV_HALO, B_WIDTH), lambda bi, ti: (bi, (ti + 1) * hb, 0))]
        + [full(x.shape) for x in conv],
        out_specs=pl.BlockSpec((None, tm, A_WIDTH + B_WIDTH), lambda bi, ti: (bi, ti, 0)),
        out_shape=jax.ShapeDtypeStruct((b, t, A_WIDTH + B_WIDTH), _F32),
        scratch_shapes=[pltpu.VMEM((tm + _CONV_HALO, B_WIDTH), _F32)],
        compiler_params=_cparams("parallel", "parallel"),
        name="mix_out",
    )(o, bon, g, *params, xx, xx, *conv)


ATT_TQ = 128
ATT_TB = ATT_TQ * max(C_DILATIONS)


def _attn_seq_kernel(*refs):
    ng = C_GROUPS
    ins = [refs[5 * g:5 * g + 5] for g in range(ng)]
    o_ref = refs[5 * ng]
    kw_refs = refs[5 * ng + 1:5 * ng + 1 + ng]
    vw_refs = refs[5 * ng + 1 + ng:5 * ng + 1 + 2 * ng]
    og_ref, lg_ref = refs[5 * ng + 1 + 2 * ng:]
    ti = pl.program_id(2)
    tq = ATT_TQ
    row = lax.broadcasted_iota(jnp.int32, (tq, 2 * tq), 0)
    col = lax.broadcasted_iota(jnp.int32, (tq, 2 * tq), 1)
    diff = tq + row - col
    band = (diff >= 0) & (diff <= C_NKEYS - 1)
    for g in range(ng):
        d = C_DILATIONS[g]
        q_ref, k_ref, kp_ref, v_ref, vp_ref = ins[g]
        kw, vw = kw_refs[g], vw_refs[g]
        halo = tq * d
        kw[:halo] = kp_ref[...]
        kw[halo:] = k_ref[...]
        vw[:halo] = vp_ref[...]
        vw[halo:] = v_ref[...]
        nsb = ATT_TB // halo

        def tile(n, carry, d=d, g=g, kw=kw, vw=vw, q_ref=q_ref, nsb=nsb):
            res = n % d
            sb = n // d
            start = res + sb * (tq * d)
            if d == 1:
                start = pl.multiple_of(start, tq)
                q = q_ref[pl.ds(start, tq), :]
                kcat = kw[pl.ds(start, 2 * tq), :]
                vcat = vw[pl.ds(start, 2 * tq), :]
            else:
                q = q_ref[pl.ds(start, tq, stride=d), :]
                kcat = kw[pl.ds(start, 2 * tq, stride=d), :]
                vcat = vw[pl.ds(start, 2 * tq, stride=d), :]
            s = _dot_nt(q, kcat) * C_SCALE
            valid = band & ((col >= tq) | (sb > 0) | (ti > 0))
            s = jnp.where(valid, s, -jnp.inf)
            m = jnp.max(s, axis=-1, keepdims=True)
            p = jnp.exp(s - m)
            l = jnp.sum(p, axis=-1, keepdims=True)
            o = _dot(p, vcat) / l
            lse = jnp.broadcast_to(m + jnp.log(l), (tq, C_HEAD))
            if d == 1:
                og_ref[g, pl.ds(start, tq), :] = o
                lg_ref[g, pl.ds(start, tq), :] = lse
            else:
                og_ref[g, pl.ds(start, tq, stride=d), :] = o
                lg_ref[g, pl.ds(start, tq, stride=d), :] = lse
            return carry

        lax.fori_loop(0, d * nsb, tile, 0)
    a0, a1, a2 = lg_ref[0], lg_ref[1], lg_ref[2]
    m = jnp.maximum(jnp.maximum(a0, a1), a2)
    e0, e1, e2 = jnp.exp(a0 - m), jnp.exp(a1 - m), jnp.exp(a2 - m)
    o_ref[...] = (e0 * og_ref[0] + e1 * og_ref[1] + e2 * og_ref[2]) / (e0 + e1 + e2)


def _attn_seq(qkv):
    b, t, n = qkv.shape
    tb = ATT_TB
    in_specs, scratch_k, scratch_v = [], [], []
    for g in range(C_GROUPS):
        halo = ATT_TQ * C_DILATIONS[g]
        per = tb // halo
        colblk = lambda s, g=g: (s * C_GROUPS + g) * C_HEADS
        cur = lambda s, g=g: pl.BlockSpec(
            (None, tb, C_HEAD), lambda bi, hi, ti, c=colblk(s): (bi, ti, c + hi))
        prv = lambda s, g=g, halo=halo, per=per: pl.BlockSpec(
            (None, halo, C_HEAD),
            lambda bi, hi, ti, c=colblk(s): (bi, jnp.maximum(ti * per - 1, 0), c + hi))
        in_specs += [cur(0), cur(1), prv(1), cur(2), prv(2)]
        scratch_k.append(pltpu.VMEM((tb + halo, C_HEAD), _F32))
        scratch_v.append(pltpu.VMEM((tb + halo, C_HEAD), _F32))
    return pl.pallas_call(
        _attn_seq_kernel,
        grid=(b, C_HEADS, t // tb),
        in_specs=in_specs,
        out_specs=pl.BlockSpec((None, tb, C_HEAD), lambda bi, hi, ti: (bi, ti, hi)),
        out_shape=jax.ShapeDtypeStruct((b, t, C_WIDTH), _F32),
        scratch_shapes=scratch_k + scratch_v + [pltpu.VMEM((C_GROUPS, tb, C_HEAD), _F32),
                                               pltpu.VMEM((C_GROUPS, tb, C_HEAD), _F32)],
        compiler_params=_cparams("parallel", "parallel", "parallel"),
        name="attn_seq",
    )(*([qkv] * (5 * C_GROUPS)))


def _attn_dec_kernel(q_ref, kn_ref, vn_ref, c0_ref, c1_ref, c2_ref, o_ref, *, t_new):
    for h in range(C_HEADS):
        o_ref[:, h * C_HEAD:(h + 1) * C_HEAD] = _attn_dec_head(
            q_ref, kn_ref, vn_ref, (c0_ref, c1_ref, c2_ref), h, t_new)


def _attn_dec_head(q_ref, kn_ref, vn_ref, c_refs, h, t_new):
    tp = q_ref.shape[1]
    hs = slice(h * C_HEAD, (h + 1) * C_HEAD)
    outs, lses = [], []
    for gi, c_ref in enumerate(c_refs):
        d = C_DILATIONS[gi]
        L = c_ref.shape[1]
        q = q_ref[gi, :, hs]
        kn = kn_ref[gi, :, hs]
        vn = vn_ref[gi, :, hs]
        s_c = _dot_nt(q, c_ref[0, :, h, :]) * C_SCALE
        qi = lax.broadcasted_iota(jnp.int32, (tp, L), 0)
        kj = lax.broadcasted_iota(jnp.int32, (tp, L), 1)
        dist = L + qi - kj
        ok_c = (dist % d == 0) & (dist <= (C_NKEYS - 1) * d)
        s_c = jnp.where(ok_c, s_c, -jnp.inf)
        m = jnp.max(s_c, axis=-1, keepdims=True)
        qrow = lax.broadcasted_iota(jnp.int32, (tp, 1), 0)
        s_n = []
        for j in range(t_new):
            dn = qrow - j
            ok = (dn >= 0) & (dn % d == 0) & (dn <= (C_NKEYS - 1) * d)
            col = jnp.sum(q * kn[j:j + 1], axis=-1, keepdims=True) * C_SCALE
            s_n.append(jnp.where(ok, col, -jnp.inf))
            m = jnp.maximum(m, s_n[j])
        p_c = jnp.exp(s_c - m)
        l = jnp.sum(p_c, axis=-1, keepdims=True)
        o = _dot(p_c, c_ref[1, :, h, :])
        for j in range(t_new):
            p_j = jnp.exp(s_n[j] - m)
            l = l + p_j
            o = o + p_j * vn[j:j + 1]
        outs.append(o / l)
        lses.append(m + jnp.log(l))
    m = jnp.maximum(jnp.maximum(lses[0], lses[1]), lses[2])
    es = [jnp.exp(x - m) for x in lses]
    return (es[0] * outs[0] + es[1] * outs[1] + es[2] * outs[2]) / (es[0] + es[1] + es[2])


def _attn_dec(qkv, caches, layer, t_new):
    b, tp = qkv.shape[:2]
    x = jnp.transpose(qkv, (2, 0, 3, 1, 4))
    new = lambda s: pl.BlockSpec((None, None, C_GROUPS, tp, C_WIDTH),
                                 lambda bi: (s, bi, 0, 0, 0))
    cspec = lambda c: _once((None, 2, None) + c.shape[3:], lambda bi: (layer, 0, bi, 0, 0, 0))
    return pl.pallas_call(
        functools.partial(_attn_dec_kernel, t_new=t_new),
        grid=(b,),
        in_specs=[new(0), new(1), new(2)] + [cspec(c) for c in caches],
        out_specs=pl.BlockSpec((None, tp, C_WIDTH), lambda bi: (bi, 0, 0)),
        out_shape=jax.ShapeDtypeStruct((b, tp, C_WIDTH), _F32),
        compiler_params=_cparams("parallel"),
        name="attn_decode",
    )(x, x, x, *caches)


def _pad_rows(x, n, axis=1):
    pad = [(0, 0)] * x.ndim
    pad[axis] = (0, n - x.shape[axis])
    return jnp.pad(x, pad)


SEQ_TM = 1024
SEQ_TM_PREP = 256
SEQ_TM_MIX = 128
SEQ_CHUNK = 64
DEC_CHUNK = 32
FFN_TF = 256
SEQ_SCAN_BS = 2
DEC_SCAN_BS = 2


def _trunk(x, mods, shift_s, wkv_s, convb_s, ffn_s, kv_s, W, decode):
    bn, t, d = x.shape
    r = bn * t
    xf = x.reshape(r, d)
    if decode:
        tm = r
        bpb = 1
        modv = lambda l, k: jnp.repeat(mods[l, :, k], t, axis=0)[None]
    else:
        tm = min(SEQ_TM, t)
        bpb = t // tm
        modv = lambda l, k: mods[l, :, k][:, None, :]
    new_shift, new_wkv, new_convb, new_ffn = [], [], [], []
    new_kv = [[] for _ in C_WINDOWS]
    a3 = 3 * A_WIDTH
    for l in range(DEPTH):
        if l % 2 == 0:
            e = l // 2
            n_in = W['ab_w_in'].shape[2]
            p, h = _premix(xf, W['g_pre_mix'][l], modv(l, 0), modv(l, 1), W['ab_w_in'][e],
                           n_in, tm, 512, bpb, emit_h=True)
            new_shift.append(h.reshape(bn, t, d)[:, -1])
            if decode:
                sp = shift_s[e]
                pa_st = _plain_mm(sp, W['ab_w_in'][e], a3, 512)
                place = lambda s: jnp.zeros((bn, t, s.shape[-1]), _F32).at[:, 0].set(s).reshape(r, -1)
                outs = _rwkv_prep(h, p, W, e, tm, bpb, first=(place(sp), place(pa_st)), seg=t)
            else:
                tmp = min(SEQ_TM_PREP, t)
                outs = _rwkv_prep(h, p, W, e, tmp, t // tmp,
                                  state=(jnp.zeros((bn, 1, d), _F32), jnp.zeros((bn, 1, a3), _F32)))
            rr, lw, k2, vv, av, bv, gg, bon, u = [o.reshape(bn, t, -1) for o in outs]
            if decode:
                chunk = DEC_CHUNK
                scan_in = [_pad_rows(a_, chunk) for a_ in (rr, k2, vv, lw, av, bv)]
                z0 = _state_to_bd(wkv_s[e].astype(_F32))
            else:
                chunk = SEQ_CHUNK
                scan_in = [rr, k2, vv, lw, av, bv]
                z0 = jnp.zeros((bn, SCAN_NG, SCAN_GL, SCAN_GL), _F32)
            o, zf = _wkv_scan(*scan_in, z0, chunk, DEC_SCAN_BS if decode else SEQ_SCAN_BS)
            new_wkv.append(_bd_to_state(zf))
            cprev = convb_s[e] if decode else jnp.zeros((bn, B_CONV - 1, B_WIDTH), _F32)
            xx = jnp.concatenate([cprev, u], axis=1)
            new_convb.append(xx[:, t:])
            if decode:
                tp = _CONV_HALO
                mo = _mix_out(_pad_rows(o[:, :t], tp), _pad_rows(bon, tp), _pad_rows(gg, tp),
                              _pad_rows(xx, tp + _CONV_HALO), W, e, tp)[:, :t]
            else:
                mo = _mix_out(o, bon, gg, _pad_rows(xx, t + _CONV_HALO), W, e, SEQ_TM_MIX)
            xf = _post_mm(mo.reshape(r, -1), W['ab_w_out'][e], xf, W['g_post_mix'][l], modv(l, 2),
                          tm, 512, bpb)
        else:
            oi = l // 2
            n_qkv = W['attn_w_qkv'].shape[2]
            qkv, = _premix(xf, W['g_pre_mix'][l], modv(l, 0), modv(l, 1), W['attn_w_qkv'][oi],
                           n_qkv, tm, 512, bpb, emit_h=False)
            q5 = qkv.reshape(bn, t, 3, C_GROUPS, C_HEADS, C_HEAD)
            for gi in range(C_GROUPS):
                keep = min(C_WINDOWS[gi], t)
                new_kv[gi].append(jnp.stack([q5[:, t - keep:, 1, gi], q5[:, t - keep:, 2, gi]]))
            if decode:
                tp = SUBLANES
                qp = _pad_rows(qkv.reshape(bn, t, 3, C_GROUPS, C_WIDTH), tp)
                att = _attn_dec(qp, kv_s, oi, t)[:, :t].reshape(r, C_WIDTH)
            else:
                att = _attn_seq(qkv.reshape(bn, t, n_qkv)).reshape(r, C_WIDTH)
            xf = _post_mm(att, W['attn_w_out'][oi], xf, W['g_post_mix'][l], modv(l, 2), tm, 512, bpb)
        if decode:
            st = ffn_s[l]
            f = st.shape[-1]
            p1 = jnp.zeros((bn, t, f), _F32).at[:, 0].set(st[:, 1]).reshape(r, f)
            p2 = jnp.zeros((bn, t, f), _F32).at[:, 0].set(st[:, 0]).at[:, 1].set(st[:, 1]).reshape(r, f)
            xf, gt = _ffn(xf, W['g_pre_ffn'][l], modv(l, 3), modv(l, 4), W['ffn_w_gate'][l],
                          W['ffn_w_up'][l], W['ffn_conv_w'][l], W['ffn_conv_b'][l], W['ffn_w_down'][l],
                          W['g_post_ffn'][l], modv(l, 5), tm, FFN_TF, bpb, prev=(p1, p2), seg=t)
            new_ffn.append(gt.reshape(bn, t, f)[:, t - 2:])
        else:
            xf, gt = _ffn(xf, W['g_pre_ffn'][l], modv(l, 3), modv(l, 4), W['ffn_w_gate'][l],
                          W['ffn_w_up'][l], W['ffn_conv_w'][l], W['ffn_conv_b'][l], W['ffn_w_down'][l],
                          W['g_post_ffn'][l], modv(l, 5), tm, FFN_TF, bpb)
            f = gt.shape[-1]
            new_ffn.append(gt.reshape(bn, bpb, SUBLANES, f)[:, -1, SUBLANES - 2:])
    return (xf.reshape(bn, t, d), jnp.stack(new_shift), jnp.stack(new_wkv), jnp.stack(new_convb),
            jnp.stack(new_ffn), jnp.stack(new_kv[0]), jnp.stack(new_kv[1]), jnp.stack(new_kv[2]))


def kernel(x_prompt, x_sample, state_shift, state_wkv, state_conv_b, state_ffn, cache_kv_w128, cache_kv_w512, cache_kv_w2048, c_prompt, c_sample, w_mod, b_mod, g_pre_mix, g_post_mix, g_pre_ffn, g_post_ffn, ab_w_in, a_mu_rkv, a_mu_wag, a_w0, a_w1, a_w2, a_a0, a_a1, a_a2, a_g1, a_g2, a_k_k, a_k_a, a_r_k, a_ln_w, a_ln_b, b_conv_w, b_conv_b, b_ln_w, b_ln_b, ab_w_out, attn_w_qkv, attn_w_out, ffn_w_gate, ffn_w_up, ffn_conv_w, ffn_conv_b, ffn_w_down):
    W = dict(w_mod=w_mod, b_mod=b_mod, g_pre_mix=g_pre_mix, g_post_mix=g_post_mix,
             g_pre_ffn=g_pre_ffn, g_post_ffn=g_post_ffn, ab_w_in=ab_w_in, a_mu_rkv=a_mu_rkv,
             a_mu_wag=a_mu_wag, a_w0=a_w0, a_w1=a_w1, a_w2=a_w2, a_a0=a_a0, a_a1=a_a1, a_a2=a_a2,
             a_g1=a_g1, a_g2=a_g2, a_k_k=a_k_k, a_k_a=a_k_a, a_r_k=a_r_k, a_ln_w=a_ln_w, a_ln_b=a_ln_b,
             b_conv_w=b_conv_w, b_conv_b=b_conv_b, b_ln_w=b_ln_w, b_ln_b=b_ln_b, ab_w_out=ab_w_out,
             attn_w_qkv=attn_w_qkv, attn_w_out=attn_w_out, ffn_w_gate=ffn_w_gate, ffn_w_up=ffn_w_up,
             ffn_conv_w=ffn_conv_w, ffn_conv_b=ffn_conv_b, ffn_w_down=ffn_w_down)
    nbp = x_prompt.shape[0]
    nbs = x_sample.shape[0]
    d = x_prompt.shape[-1]
    assert d == D_MODEL and w_mod.shape == (DEPTH, d, N_MOD * d)
    c_all = _pad_rows(jnp.concatenate([c_prompt, c_sample], axis=0), 16, axis=0)
    mods = _mods(c_all, w_mod, b_mod).reshape(DEPTH, 16, N_MOD, d)
    outs_p = _trunk(x_prompt, mods[:, :nbp], None, None, None, None, None, W, decode=False)
    outs_s = _trunk(x_sample, mods[:, nbp:nbp + nbs], state_shift, state_wkv, state_conv_b,
                    state_ffn, (cache_kv_w128, cache_kv_w512, cache_kv_w2048), W, decode=True)
    (y_p, p_shift, p_wkv, p_conv_b, p_ffn, p_kv0, p_kv1, p_kv2) = outs_p
    (y_s, s_shift, s_wkv, s_conv_b, s_ffn, s_kv0, s_kv1, s_kv2) = outs_s
    return (y_p, y_s, p_shift, p_wkv, p_conv_b, p_ffn, p_kv0, p_kv1, p_kv2,
            s_shift, s_wkv, s_conv_b, s_ffn, s_kv0, s_kv1, s_kv2)
```

```python
import functools

import jax
import jax.numpy as jnp
from jax import lax
from jax.experimental import pallas as pl
from jax.experimental.pallas import tpu as pltpu

_BF = jnp.bfloat16
_F32 = jnp.float32

D_MODEL = 2048
DEPTH = 4
N_MOD = 6
A_HEAD = 64
A_HEADS = 16
A_WIDTH = A_HEADS * A_HEAD
A_GN_EPS = 64e-5
B_WIDTH = 1024
B_CONV = 31
C_WINDOWS = (128, 512, 2048)
C_DILATIONS = (1, 4, 16)
C_GROUPS = 3
C_HEADS = 8
C_HEAD = 128
C_WIDTH = C_HEADS * C_HEAD
C_SCALE = C_HEAD ** -0.5
C_NKEYS = 129
D_FF = 5632
RMS_EPS = 1e-6
LN_EPS = 1e-5

SUBLANES = 8
LANES = 128
VMEM_LIMIT = 56 * 1024 * 1024

SCAN_G = 4
SCAN_GL = SCAN_G * A_HEAD
SCAN_NG = A_HEADS // SCAN_G


def _cparams(*sem):
    return pltpu.CompilerParams(dimension_semantics=sem, vmem_limit_bytes=VMEM_LIMIT)


def _dot(a, b):
    return jnp.dot(a.astype(_BF), b.astype(_BF), preferred_element_type=_F32)


def _dot_nt(a, b):
    return lax.dot_general(a.astype(_BF), b.astype(_BF), (((1,), (1,)), ((), ())),
                           preferred_element_type=_F32)


def _dot_tn(a, b):
    return lax.dot_general(a.astype(_BF), b.astype(_BF), (((0,), (0,)), ((), ())),
                           preferred_element_type=_F32)


def _split3(x):
    h1 = x.astype(_BF)
    r1 = x - h1.astype(_F32)
    h2 = r1.astype(_BF)
    h3 = (r1 - h2.astype(_F32)).astype(_BF)
    return h1, h2, h3


def _dot_exact_rhs(x, m_bf):
    h1, h2, h3 = _split3(x)
    return (jnp.dot(h1, m_bf, preferred_element_type=_F32)
            + jnp.dot(h2, m_bf, preferred_element_type=_F32)
            + jnp.dot(h3, m_bf, preferred_element_type=_F32))


def _split2(x):
    hi = x.astype(_BF)
    return hi, (x - hi.astype(_F32)).astype(_BF)


def _dot_hilo(lhs_list, rhs):
    his, los = zip(*[_split2(a) for a in lhs_list])
    rh, rl = _split2(rhs)
    y1 = jnp.dot(jnp.concatenate(his + los, axis=0), rh, preferred_element_type=_F32)
    y2 = jnp.dot(jnp.concatenate(his, axis=0), rl, preferred_element_type=_F32)
    tot = sum(a.shape[0] for a in lhs_list)
    outs, off = [], 0
    for a in lhs_list:
        m = a.shape[0]
        outs.append(y1[off:off + m] + y1[tot + off:tot + off + m] + y2[off:off + m])
        off += m
    return outs


def _sigmoid(x):
    return 1.0 / (1.0 + jnp.exp(-x))


def _silu(x):
    return x * _sigmoid(x)


def _rms(y, g):
    return y * lax.rsqrt(jnp.mean(y * y, axis=-1, keepdims=True) + RMS_EPS) * g


def _mod_kernel(c_ref, w_ref, b_ref, o_ref):
    c = c_ref[...]
    ch = c.astype(_BF)
    cl = (c - ch.astype(_F32)).astype(_BF)
    y = jnp.dot(jnp.concatenate([ch, cl], axis=0), w_ref[...].astype(_BF),
                preferred_element_type=_F32)
    n = c.shape[0]
    o_ref[...] = y[:n] + y[n:] + b_ref[...]


def _mods(c_all, w_mod, b_mod):
    depth, d, n = w_mod.shape
    rows = c_all.shape[0]
    tn = 1024
    return pl.pallas_call(
        _mod_kernel,
        grid=(depth, n // tn),
        in_specs=[pl.BlockSpec((rows, d), lambda l, j: (0, 0)),
                  pl.BlockSpec((None, d, tn), lambda l, j: (l, 0, j)),
                  pl.BlockSpec((None, 1, tn), lambda l, j: (l, 0, j))],
        out_specs=pl.BlockSpec((None, rows, tn), lambda l, j: (l, 0, j)),
        out_shape=jax.ShapeDtypeStruct((depth, rows, n), _F32),
        compiler_params=_cparams("parallel", "parallel"),
        name="mods",
    )(c_all, w_mod, b_mod.reshape(depth, 1, n))


def _premix_kernel(x_ref, g_ref, sh_ref, sc_ref, w_ref, o_ref, *rest):
    hs_ref = rest[-1]

    @pl.when(pl.program_id(1) == 0)
    def _():
        h = _rms(x_ref[...], g_ref[...]) * (1.0 + sc_ref[...]) + sh_ref[...]
        if len(rest) == 2:
            rest[0][...] = h
        hs_ref[...] = h.astype(_BF)

    o_ref[...] = jnp.dot(hs_ref[...], w_ref[...].astype(_BF), preferred_element_type=_F32)


def _mod_spec(mod, tm, bpb):
    ms = mod.shape[1]
    d = mod.shape[2]
    if ms == 1:
        return pl.BlockSpec((None, 1, d), lambda i, *_: (i // bpb, 0, 0))
    assert mod.shape[0] == 1 and ms == tm
    return pl.BlockSpec((None, ms, d), lambda i, *_: (0, 0, 0))


def _once(block_shape, index_map):
    return pl.BlockSpec(block_shape, index_map, pipeline_mode=pl.Buffered(1))


def _premix(x, g, shift, scale, w, n_out, tm, tn, bpb, emit_h):
    r, d = x.shape
    w, wl = w
    out_specs = [pl.BlockSpec((tm, tn), lambda i, j: (i, j))]
    out_shape = [jax.ShapeDtypeStruct((r, n_out), _F32)]
    if emit_h:
        out_specs.append(pl.BlockSpec((tm, d), lambda i, j: (i, 0)))
        out_shape.append(jax.ShapeDtypeStruct((r, d), _F32))
    return pl.pallas_call(
        _premix_kernel,
        grid=(r // tm, n_out // tn),
        in_specs=[_once((tm, d), lambda i, j: (i, 0)),
                  pl.BlockSpec((1, d), lambda i, j: (0, 0)),
                  _mod_spec(shift, tm, bpb), _mod_spec(scale, tm, bpb),
                  pl.BlockSpec((None, d, tn), lambda i, j: (wl, 0, j))],
        out_specs=out_specs,
        out_shape=out_shape,
        scratch_shapes=[pltpu.VMEM((tm, d), _BF)],
        compiler_params=_cparams("parallel", "arbitrary"),
        name="premix_matmul",
    )(x, g.reshape(1, d), shift, scale, w)


def _plain_mm_kernel(a_ref, w_ref, o_ref):
    o_ref[...] = _dot(a_ref[...], w_ref[...])


def _plain_mm(a, w, n_out, tn):
    m, k = a.shape
    w, wl = w
    return pl.pallas_call(
        _plain_mm_kernel,
        grid=(n_out // tn,),
        in_specs=[pl.BlockSpec((m, k), lambda j: (0, 0)),
                  pl.BlockSpec((None, k, tn), lambda j: (wl, 0, j))],
        out_specs=pl.BlockSpec((m, tn), lambda j: (0, j)),
        out_shape=jax.ShapeDtypeStruct((m, n_out), _F32),
        compiler_params=_cparams("parallel"),
        name="plain_matmul",
    )(a, w)


def _post_kernel(a_ref, w_ref, x_ref, g_ref, gm_ref, o_ref):
    k = pl.program_id(1)

    @pl.when(k == 0)
    def _():
        o_ref[...] = jnp.zeros_like(o_ref)

    o_ref[...] += _dot(a_ref[...], w_ref[...])

    @pl.when(k == pl.num_programs(1) - 1)
    def _():
        o_ref[...] = x_ref[...] + gm_ref[...] * _rms(o_ref[...], g_ref[...])


def _post_mm(a, w, x, g, gate, tm, tk, bpb):
    r, kdim = a.shape
    w, wl = w
    d = w.shape[2]
    return pl.pallas_call(
        _post_kernel,
        grid=(r // tm, kdim // tk),
        in_specs=[pl.BlockSpec((tm, tk), lambda i, k: (i, k)),
                  pl.BlockSpec((None, tk, d), lambda i, k: (wl, k, 0)),
                  _once((tm, d), lambda i, k: (i, 0)),
                  pl.BlockSpec((1, d), lambda i, k: (0, 0)),
                  _mod_spec(gate, tm, bpb)],
        out_specs=pl.BlockSpec((tm, d), lambda i, k: (i, 0)),
        out_shape=jax.ShapeDtypeStruct((r, d), _F32),
        compiler_params=_cparams("parallel", "arbitrary"),
        name="post_matmul",
    )(a, w, x, g.reshape(1, d), gate)


FFN_ROWS = 64


def _ffn_kernel(*refs, seq_mode, bpb, seg):
    if seq_mode:
        (x_ref, gpre_ref, sh_ref, sc_ref, wg_ref, wu_ref, cw_ref, cb_ref, wd_ref, gpost_ref,
         gm_ref, o_ref, gt_ref, hs_ref, carry_ref, gate_ref, up_ref, act_ref) = refs
    else:
        (x_ref, gpre_ref, sh_ref, sc_ref, wg_ref, wu_ref, cw_ref, cb_ref, wd_ref, gpost_ref,
         gm_ref, p1_ref, p2_ref, o_ref, gt_ref, hs_ref) = refs
    i = pl.program_id(0)
    j = pl.program_id(1)

    @pl.when(j == 0)
    def _():
        h = _rms(x_ref[...], gpre_ref[...]) * (1.0 + sc_ref[...]) + sh_ref[...]
        hs_ref[...] = h.astype(_BF)
        o_ref[...] = jnp.zeros_like(o_ref)

    hs = hs_ref[...]
    tm = hs.shape[0]
    cw = cw_ref[...]
    cb = cb_ref[...]
    if seq_mode:
        hdr = SUBLANES

        @pl.when(i % bpb == 0)
        def _():
            carry_ref[j] = jnp.zeros((hdr, gate_ref.shape[1]), _F32)

        gate_ref[:hdr] = carry_ref[j]
        gate_ref[hdr:] = jnp.dot(hs, wg_ref[...].astype(_BF), preferred_element_type=_F32)
        up_ref[...] = jnp.dot(hs, wu_ref[...].astype(_BF), preferred_element_type=_F32)
        tail = gate_ref[tm:]
        carry_ref[j] = tail
        gt_ref[...] = tail
        for c in range(tm // FFN_ROWS):
            r0 = hdr + c * FFN_ROWS
            gc = (cw[0:1] * gate_ref[r0 - 2:r0 - 2 + FFN_ROWS] + cw[1:2] * gate_ref[r0 - 1:r0 - 1 + FFN_ROWS]
                  + cw[2:3] * gate_ref[r0:r0 + FFN_ROWS] + cb)
            act_ref[c * FFN_ROWS:(c + 1) * FFN_ROWS] = (
                _silu(gc) * up_ref[c * FFN_ROWS:(c + 1) * FFN_ROWS]).astype(_BF)
        act = act_ref[...]
    else:
        gate = jnp.dot(hs, wg_ref[...].astype(_BF), preferred_element_type=_F32)
        up = jnp.dot(hs, wu_ref[...].astype(_BF), preferred_element_type=_F32)
        rs = lax.broadcasted_iota(jnp.int32, gate.shape, 0) % seg
        g1 = jnp.where(rs == 0, p1_ref[...], pltpu.roll(gate, 1, axis=0))
        g2 = jnp.where(rs < 2, p2_ref[...], pltpu.roll(gate, 2, axis=0))
        gt_ref[...] = gate
        act = (_silu(cw[0:1] * g2 + cw[1:2] * g1 + cw[2:3] * gate + cb) * up).astype(_BF)
    o_ref[...] += jnp.dot(act, wd_ref[...].astype(_BF), preferred_element_type=_F32)

    @pl.when(j == pl.num_programs(1) - 1)
    def _():
        o_ref[...] = x_ref[...] + gm_ref[...] * _rms(o_ref[...], gpost_ref[...])


def _ffn(x, gpre, shift, scale, wg, wu, cw, cb, wd, gpost, gate, tm, tf, bpb, prev=None, seg=1):
    r, d = x.shape
    (wg, lg), (wu, lu), (wd, ld) = wg, wu, wd
    f = wg.shape[2]
    nblk = r // tm
    seq_mode = prev is None
    gt_rows = SUBLANES if seq_mode else tm
    in_specs = [_once((tm, d), lambda i, j: (i, 0)),
                pl.BlockSpec((1, d), lambda i, j: (0, 0)),
                _mod_spec(shift, tm, bpb), _mod_spec(scale, tm, bpb),
                pl.BlockSpec((None, d, tf), lambda i, j: (lg, 0, j)),
                pl.BlockSpec((None, d, tf), lambda i, j: (lu, 0, j)),
                pl.BlockSpec((cw.shape[0], tf), lambda i, j: (0, j)),
                pl.BlockSpec((1, tf), lambda i, j: (0, j)),
                pl.BlockSpec((None, tf, d), lambda i, j: (ld, j, 0)),
                pl.BlockSpec((1, d), lambda i, j: (0, 0)),
                _mod_spec(gate, tm, bpb)]
    args = [x, gpre.reshape(1, d), shift, scale, wg, wu, cw, cb.reshape(1, f), wd,
            gpost.reshape(1, d), gate]
    scratch = [pltpu.VMEM((tm, d), _BF)]
    if seq_mode:
        assert tm % FFN_ROWS == 0
        scratch += [pltpu.VMEM((f // tf, SUBLANES, tf), _F32),
                    pltpu.VMEM((SUBLANES + tm, tf), _F32), pltpu.VMEM((tm, tf), _F32),
                    pltpu.VMEM((tm, tf), _BF)]
    else:
        in_specs += [pl.BlockSpec((tm, tf), lambda i, j: (i, j))] * 2
        args += list(prev)
    return pl.pallas_call(
        functools.partial(_ffn_kernel, seq_mode=seq_mode, bpb=bpb, seg=seg),
        grid=(nblk, f // tf),
        in_specs=in_specs,
        out_specs=[_once((tm, d), lambda i, j: (i, 0)),
                   pl.BlockSpec((None, gt_rows, tf), lambda i, j: (i, 0, j))],
        out_shape=[jax.ShapeDtypeStruct((r, d), _F32),
                   jax.ShapeDtypeStruct((nblk, gt_rows, f), _F32)],
        scratch_shapes=scratch,
        compiler_params=_cparams("arbitrary", "arbitrary"),
        name="conv_ffn",
    )(*args)


def _prev_rows(cur, first_row_of):
    return first_row_of(pltpu.roll(cur, 1, axis=0))


def _rwkv_prep_kernel(*refs, seq_mode, bpb, seg):
    if seq_mode:
        (h_ref, h8_ref, hst_ref, pa_ref, pa8_ref, past_ref, pb1_ref, pb2_ref) = refs[:8]
        rest = refs[8:]
    else:
        (h_ref, hf_ref, pa_ref, paf_ref, pb1_ref, pb2_ref) = refs[:6]
        rest = refs[6:]
    (murkv_ref, muwag_ref, w0_ref, w1_ref, w2_ref, a0_ref, a1_ref, a2_ref, g1_ref, g2_ref,
     kk_ref, ka_ref, rk_ref, ones_ref,
     r_out, lw_out, k_out, v_out, av_out, bv_out, g_out, bon_out, u_out) = rest
    i = pl.program_id(0)
    h = h_ref[...]
    pa = pa_ref[...]
    rows_h = lax.broadcasted_iota(jnp.int32, h.shape, 0)
    rows_p = lax.broadcasted_iota(jnp.int32, pa.shape, 0)
    hp = pltpu.roll(h, 1, axis=0)
    pp = pltpu.roll(pa, 1, axis=0)
    if seq_mode:
        first = i % bpb == 0
        h0 = jnp.where(first, hst_ref[...], h8_ref[SUBLANES - 1:SUBLANES])
        p0 = jnp.where(first, past_ref[...], pa8_ref[SUBLANES - 1:SUBLANES])
        hp = jnp.where(rows_h == 0, h0, hp)
        pp = jnp.where(rows_p == 0, p0, pp)
    else:
        hp = jnp.where(rows_h % seg == 0, hf_ref[...], hp)
        pp = jnp.where(rows_p % seg == 0, paf_ref[...], pp)
    delta = hp - h
    mu = muwag_ref[...]
    xw = h + delta * mu[0:1]
    xa = h + delta * mu[1:2]
    xg = h + delta * mu[2:3]
    zw = w0_ref[...] + _dot(jnp.tanh(_dot(xw, w1_ref[...])), w2_ref[...])
    w_log = -(jnp.maximum(-zw, 0.0) + jnp.log(1.0 + jnp.exp(-jnp.abs(zw)))) - 0.5
    lw_out[...] = -jnp.exp(w_log)
    a = _sigmoid(a0_ref[...] + _dot(_dot(xa, a1_ref[...]), a2_ref[...]))
    g_out[...] = _dot(_sigmoid(_dot(xg, g1_ref[...])), g2_ref[...])
    rkv = pa + (pp - pa) * murkv_ref[...]
    r = rkv[:, :A_WIDTH]
    k = rkv[:, A_WIDTH:2 * A_WIDTH]
    v = rkv[:, 2 * A_WIDTH:]
    ones = ones_ref[...]

    def segsum(x):
        parts = [_dot_exact_rhs(x[:, c * SCAN_GL:(c + 1) * SCAN_GL], ones)
                 for c in range(A_WIDTH // SCAN_GL)]
        return jnp.concatenate(parts, axis=1)

    kk = k * kk_ref[...]
    kk = kk * lax.rsqrt(jnp.maximum(segsum(kk * kk), 1e-24))
    k2 = k * (1.0 + (a - 1.0) * ka_ref[...])
    r_out[...] = r
    k_out[...] = k2
    v_out[...] = v
    av_out[...] = -kk
    bv_out[...] = kk * a
    bon_out[...] = segsum(r * k2 * rk_ref[...]) * v
    u_out[...] = pb1_ref[...] * _sigmoid(pb2_ref[...])


def _head_ones():
    idx = jnp.arange(SCAN_GL) // A_HEAD
    return (idx[:, None] == idx[None, :]).astype(_BF)


def _rwkv_prep(h, p, W, e, tm, bpb, state=None, first=None, seg=1):
    r, d = h.shape
    a3 = 3 * A_WIDTH
    seq_mode = first is None
    row = lambda n: pl.BlockSpec((tm, n), lambda i: (i, 0))
    full = lambda s: pl.BlockSpec(s, lambda i: (0,) * len(s))
    if seq_mode:
        t8 = tm // SUBLANES
        prev8 = lambda n: pl.BlockSpec((SUBLANES, n), lambda i: (jnp.maximum(i * t8 - 1, 0), 0))
        in_specs = [row(d), prev8(d), pl.BlockSpec((None, 1, d), lambda i: (i // bpb, 0, 0)),
                    row(a3), prev8(a3), pl.BlockSpec((None, 1, a3), lambda i: (i // bpb, 0, 0))]
        args = [h, h, state[0], p, p, state[1]]
    else:
        in_specs = [row(d), row(d), row(a3), row(a3)]
        args = [h, first[0], p, first[1]]
    in_specs += [pl.BlockSpec((tm, B_WIDTH), lambda i: (i, a3 // B_WIDTH)),
                 pl.BlockSpec((tm, B_WIDTH), lambda i: (i, a3 // B_WIDTH + 1))]
    args += [p, p]
    vec = lambda x: x.reshape(1, -1)
    lora = lambda n: -(-n // LANES) * LANES
    pad_c = lambda x: _pad_rows(x, lora(x.shape[1]), axis=1)
    pad_r = lambda x: _pad_rows(x, lora(x.shape[0]), axis=0)
    params = [vec(W['a_mu_rkv'][e]), W['a_mu_wag'][e], vec(W['a_w0'][e]), pad_c(W['a_w1'][e]),
              pad_r(W['a_w2'][e]), vec(W['a_a0'][e]), pad_c(W['a_a1'][e]), pad_r(W['a_a2'][e]),
              W['a_g1'][e],
              W['a_g2'][e], vec(W['a_k_k'][e]), vec(W['a_k_a'][e]), vec(W['a_r_k'][e]),
              _head_ones()]
    in_specs += [full(x.shape) for x in params]
    args += params
    out = jax.ShapeDtypeStruct((r, A_WIDTH), _F32)
    return pl.pallas_call(
        functools.partial(_rwkv_prep_kernel, seq_mode=seq_mode, bpb=bpb, seg=seg),
        grid=(r // tm,),
        in_specs=in_specs,
        out_specs=[row(A_WIDTH)] * 9,
        out_shape=[out] * 9,
        compiler_params=_cparams("parallel"),
        name="rwkv_prep",
    )(*args)


def _scan_groups(rcs, kcs, vcs, lws, avs, bvs, zbds, ltri):
    n = len(rcs)
    ix = range(n)
    c = rcs[0].shape[0]
    gc = SCAN_G * c
    cum = [_dot_exact_rhs_left(ltri, lws[i]) for i in ix]
    cl = [cum[i][c - 1:c] for i in ix]
    e_dn = [jnp.exp(-cum[i]) for i in ix]
    e_cl = [jnp.exp(cl[i] - cum[i]) for i in ix]
    rt = [rcs[i] * jnp.exp(cum[i]) for i in ix]
    at = [avs[i] * jnp.exp(cum[i] - lws[i]) for i in ix]
    kt = [kcs[i] * e_dn[i] for i in ix]
    bt = [bvs[i] * e_dn[i] for i in ix]
    kh = [kcs[i] * e_cl[i] for i in ix]
    bh = [bvs[i] * e_cl[i] for i in ix]

    row_h = lax.broadcasted_iota(jnp.int32, (gc, SCAN_GL), 0) // c
    lane_h = lax.broadcasted_iota(jnp.int32, (gc, SCAN_GL), 1) // A_HEAD
    hm_e = row_h == lane_h

    def expand(x):
        return jnp.where(hm_e, jnp.concatenate([x] * SCAN_G, axis=0), 0.0)

    bd_cc = (lax.broadcasted_iota(jnp.int32, (gc, gc), 0) // c
             == lax.broadcasted_iota(jnp.int32, (gc, gc), 1) // c)

    def bdiag(x):
        return jnp.where(bd_cc, jnp.concatenate([x] * SCAN_G, axis=0), 0.0)

    t_idx = lax.broadcasted_iota(jnp.int32, (c, gc), 0)
    j_idx = lax.broadcasted_iota(jnp.int32, (c, gc), 1) % c
    lo_s = j_idx < t_idx
    lo_i = j_idx <= t_idx
    amat = [_dot_nt(jnp.concatenate([at[i], rt[i]], axis=0),
                    jnp.concatenate([expand(kt[i]), expand(bt[i])], axis=0)) for i in ix]
    a_ak = [jnp.where(lo_s, amat[i][:c, :gc], 0.0) for i in ix]
    a_ab = [jnp.where(lo_s, amat[i][:c, gc:], 0.0) for i in ix]
    a_rk = [jnp.where(lo_i, amat[i][c:, :gc], 0.0) for i in ix]
    a_rb = [jnp.where(lo_i, amat[i][c:, gc:], 0.0) for i in ix]
    eye = jnp.where(j_idx == t_idx, 1.0, 0.0)
    trow = [eye + a_ab[i] for i in ix]
    pw = list(a_ab)
    nlev = c.bit_length() - 1
    for lev in range(nlev):
        for i in ix:
            lhs = ([trow[i]] if lev >= 1 else []) + ([pw[i]] if lev < nlev - 1 else [])
            res = _dot_hilo(lhs, bdiag(pw[i]))
            if lev >= 1:
                trow[i] = trow[i] + res[0]
            if lev < nlev - 1:
                pw[i] = res[-1]
    ve = [expand(vcs[i]) for i in ix]
    x1 = [_dot(a_ak[i], ve[i]) for i in ix]
    tu = [_dot(trow[i], jnp.concatenate([expand(x1[i]), expand(at[i])], axis=1)) for i in ix]
    u0 = [tu[i][:, :SCAN_GL] for i in ix]
    ap = [tu[i][:, SCAN_GL:] for i in ix]
    ru = [_dot(a_rb[i], jnp.concatenate([expand(ap[i]), expand(u0[i])], axis=1)) for i in ix]
    rp = [rt[i] + ru[i][:, :SCAN_GL] for i in ix]
    o0 = [_dot(a_rk[i], ve[i]) + ru[i][:, SCAN_GL:] for i in ix]
    sz = [_dot(jnp.concatenate([rp[i], ap[i]], axis=0), zbds[i]) for i in ix]
    o = [sz[i][:c] + o0[i] for i in ix]
    u = [sz[i][c:] + u0[i] for i in ix]
    dmask = (lax.broadcasted_iota(jnp.int32, (A_HEAD, SCAN_GL), 1) % A_HEAD
             == lax.broadcasted_iota(jnp.int32, (A_HEAD, SCAN_GL), 0))
    bd_ll = (lax.broadcasted_iota(jnp.int32, (SCAN_GL, SCAN_GL), 0) // A_HEAD
             == lax.broadcasted_iota(jnp.int32, (SCAN_GL, SCAN_GL), 1) // A_HEAD)
    znew = []
    for i in ix:
        zc = zbds[i][0:A_HEAD]
        for h in range(1, SCAN_G):
            zc = zc + zbds[i][h * A_HEAD:(h + 1) * A_HEAD]
        dg = jnp.where(dmask, jnp.exp(cl[i]), 0.0)
        xs = jnp.concatenate([kh[i], bh[i], dg], axis=0)
        ys = jnp.concatenate([vcs[i], u[i], zc], axis=0)
        znew.append(jnp.where(bd_ll, _dot_tn(xs, ys), 0.0))
    return o, znew


def _dot_exact_rhs_left(m_bf, x):
    h1, h2, h3 = _split3(x)
    return (jnp.dot(m_bf, h1, preferred_element_type=_F32)
            + jnp.dot(m_bf, h2, preferred_element_type=_F32)
            + jnp.dot(m_bf, h3, preferred_element_type=_F32))


def _scan_kernel(r_ref, k_ref, v_ref, lw_ref, av_ref, bv_ref, z0_ref, o_ref, zout_ref, z_ref):
    t = pl.program_id(1)

    @pl.when(t == 0)
    def _():
        z_ref[...] = z0_ref[...]

    bs, c = r_ref.shape[:2]
    ltri = (lax.broadcasted_iota(jnp.int32, (c, c), 0)
            >= lax.broadcasted_iota(jnp.int32, (c, c), 1)).astype(_BF)
    chains = [(bi, gi) for bi in range(bs) for gi in range(SCAN_NG)]
    lanes = lambda gi: slice(gi * SCAN_GL, (gi + 1) * SCAN_GL)
    tok = lambda ref: [ref[bi, :, lanes(gi)] for bi, gi in chains]
    o, znew = _scan_groups(tok(r_ref), tok(k_ref), tok(v_ref), tok(lw_ref), tok(av_ref),
                           tok(bv_ref), [z_ref[bi, gi] for bi, gi in chains], ltri)
    for n, (bi, gi) in enumerate(chains):
        o_ref[bi, :, lanes(gi)] = o[n]
        z_ref[bi, gi] = znew[n]

    @pl.when(t == pl.num_programs(1) - 1)
    def _():
        zout_ref[...] = z_ref[...]


def _wkv_scan(r, k, v, lw, av, bv, z0, chunk, bs):
    b, t, _ = r.shape
    tok = pl.BlockSpec((bs, chunk, A_WIDTH), lambda bi, ti: (bi, ti, 0))
    zspec = pl.BlockSpec((bs, SCAN_NG, SCAN_GL, SCAN_GL), lambda bi, ti: (bi, 0, 0, 0))
    return pl.pallas_call(
        _scan_kernel,
        grid=(b // bs, t // chunk),
        in_specs=[tok] * 6 + [zspec],
        out_specs=[tok, zspec],
        out_shape=[jax.ShapeDtypeStruct((b, t, A_WIDTH), _F32),
                   jax.ShapeDtypeStruct((b, SCAN_NG, SCAN_GL, SCAN_GL), _F32)],
        scratch_shapes=[pltpu.VMEM((bs, SCAN_NG, SCAN_GL, SCAN_GL), _F32)],
        compiler_params=_cparams("parallel", "arbitrary"),
        name="wkv7_scan",
    )(r, k, v, lw, av, bv, z0)


def _state_to_bd(s):
    b = s.shape[0]
    st = jnp.swapaxes(s, -1, -2).reshape(b, SCAN_NG, SCAN_G, A_HEAD, A_HEAD)
    eye = jnp.eye(SCAN_G, dtype=s.dtype)
    z = st[:, :, :, :, None, :] * eye[None, None, :, None, :, None]
    return z.reshape(b, SCAN_NG, SCAN_GL, SCAN_GL)


def _bd_to_state(z):
    b = z.shape[0]
    z = z.reshape(b, SCAN_NG, SCAN_G, A_HEAD, SCAN_G, A_HEAD)
    idx = jnp.arange(SCAN_G)
    st = z[:, :, idx, :, idx, :]
    st = jnp.moveaxis(st, 0, 2).reshape(b, A_HEADS, A_HEAD, A_HEAD)
    return jnp.swapaxes(st, -1, -2)


def _mix_out_kernel(o_ref, bon_ref, g_ref, lnw_ref, lnb_ref, xm_ref, xh_ref, cw_ref, cb_ref,
                    blw_ref, blb_ref, ones_ref, out_ref, win_ref):
    tm = o_ref.shape[0]
    ones = ones_ref[...]

    def segmean(x):
        parts = [_dot_exact_rhs(x[:, c * SCAN_GL:(c + 1) * SCAN_GL], ones)
                 for c in range(A_WIDTH // SCAN_GL)]
        return jnp.concatenate(parts, axis=1) * (1.0 / A_HEAD)

    o = o_ref[...]
    oc = o - segmean(o)
    var = segmean(oc * oc)
    on = oc * lax.rsqrt(var + A_GN_EPS) * lnw_ref[...] + lnb_ref[...]
    out_ref[:, :A_WIDTH] = (on + bon_ref[...]) * g_ref[...]

    win_ref[:tm] = xm_ref[...]
    win_ref[tm:] = xh_ref[...]
    cols = []
    for c in range(B_WIDTH // LANES):
        cs = slice(c * LANES, (c + 1) * LANES)
        acc = jnp.zeros((tm, LANES), _F32) + cb_ref[:, cs]
        for j in range(B_CONV):
            acc = acc + cw_ref[j:j + 1, cs] * win_ref[j:j + tm, cs]
        cols.append(acc)
    ub = jnp.concatenate(cols, axis=1)
    uc = ub - jnp.mean(ub, axis=-1, keepdims=True)
    uv = jnp.mean(uc * uc, axis=-1, keepdims=True)
    out_ref[:, A_WIDTH:] = _silu(uc * lax.rsqrt(uv + LN_EPS) * blw_ref[...] + blb_ref[...])


_CONV_HALO = 32


def _mix_out(o, bon, g, xx, W, e, tm):
    b, t, _ = o.shape
    tok = pl.BlockSpec((None, tm, A_WIDTH), lambda bi, ti: (bi, ti, 0))
    full = lambda s: pl.BlockSpec(s, lambda bi, ti: (0,) * len(s))
    hb = tm // _CONV_HALO
    vec = lambda x: x.reshape(1, -1)
    params = [vec(W['a_ln_w'][e]), vec(W['a_ln_b'][e])]
    conv = [W['b_conv_w'][e], vec(W['b_conv_b'][e]), vec(W['b_ln_w'][e]), vec(W['b_ln_b'][e]),
            _head_ones()]
    return pl.pallas_call(
        _mix_out_kernel,
        grid=(b, t // tm),
        in_specs=[tok, tok, tok] + [full(x.shape) for x in params]
        + [pl.BlockSpec((None, tm, B_WIDTH), lambda bi, ti: (bi, ti, 0)),
           pl.BlockSpec((None, _CONV_HALO, B_WIDTH), lambda bi, ti: (bi, (ti + 1) * hb, 0))]
        + [full(x.shape) for x in conv],
        out_specs=pl.BlockSpec((None, tm, A_WIDTH + B_WIDTH), lambda bi, ti: (bi, ti, 0)),
        out_shape=jax.ShapeDtypeStruct((b, t, A_WIDTH + B_WIDTH), _F32),
        scratch_shapes=[pltpu.VMEM((tm + _CONV_HALO, B_WIDTH), _F32)],
        compiler_params=_cparams("parallel", "parallel"),
        name="mix_out",
    )(o, bon, g, *params, xx, xx, *conv)


ATT_UNROLL = 4
ATT_TQ = 128
ATT_TB = ATT_TQ * max(C_DILATIONS)


def _attn_seq_kernel(*refs):
    ng = C_GROUPS
    ins = [refs[5 * g:5 * g + 5] for g in range(ng)]
    o_ref = refs[5 * ng]
    kw_refs = refs[5 * ng + 1:5 * ng + 1 + ng]
    vw_refs = refs[5 * ng + 1 + ng:5 * ng + 1 + 2 * ng]
    og_ref, lg_ref = refs[5 * ng + 1 + 2 * ng:]
    ti = pl.program_id(2)
    tq = ATT_TQ
    row = lax.broadcasted_iota(jnp.int32, (tq, 2 * tq), 0)
    col = lax.broadcasted_iota(jnp.int32, (tq, 2 * tq), 1)
    diff = tq + row - col
    band = (diff >= 0) & (diff <= C_NKEYS - 1)
    for g in range(ng):
        d = C_DILATIONS[g]
        q_ref, k_ref, kp_ref, v_ref, vp_ref = ins[g]
        kw, vw = kw_refs[g], vw_refs[g]
        halo = tq * d
        kw[:halo] = kp_ref[...]
        kw[halo:] = k_ref[...]
        vw[:halo] = vp_ref[...]
        vw[halo:] = v_ref[...]
        nsb = ATT_TB // halo

        def tiles(it, carry, d=d, g=g, kw=kw, vw=vw, q_ref=q_ref):
            us = range(ATT_UNROLL)
            ns = [it * ATT_UNROLL + u for u in us]
            sbs = [n // d for n in ns]
            starts = [n % d + (n // d) * (tq * d) for n in ns]
            if d == 1:
                starts = [pl.multiple_of(st, tq) for st in starts]
                rows = lambda st, n_rows: pl.ds(st, n_rows)
            else:
                rows = lambda st, n_rows: pl.ds(st, n_rows, stride=d)
            ss = [_dot_nt(q_ref[rows(starts[u], tq), :], kw[rows(starts[u], 2 * tq), :]) * C_SCALE
                  for u in us]
            ss = [jnp.where(band & ((col >= tq) | (sbs[u] > 0) | (ti > 0)), ss[u], -jnp.inf)
                  for u in us]
            ms = [jnp.max(ss[u], axis=-1, keepdims=True) for u in us]
            ps = [jnp.exp(ss[u] - ms[u]) for u in us]
            ls = [jnp.sum(ps[u], axis=-1, keepdims=True) for u in us]
            os_ = [_dot(ps[u], vw[rows(starts[u], 2 * tq), :]) / ls[u] for u in us]
            for u in us:
                og_ref[g, rows(starts[u], tq), :] = os_[u]
                lg_ref[g, rows(starts[u], tq), :] = jnp.broadcast_to(ms[u] + jnp.log(ls[u]),
                                                                     (tq, C_HEAD))
            return carry

        lax.fori_loop(0, d * nsb // ATT_UNROLL, tiles, 0)
    a0, a1, a2 = lg_ref[0], lg_ref[1], lg_ref[2]
    m = jnp.maximum(jnp.maximum(a0, a1), a2)
    e0, e1, e2 = jnp.exp(a0 - m), jnp.exp(a1 - m), jnp.exp(a2 - m)
    o_ref[...] = (e0 * og_ref[0] + e1 * og_ref[1] + e2 * og_ref[2]) / (e0 + e1 + e2)


def _attn_seq(qkv):
    b, t, n = qkv.shape
    tb = ATT_TB
    in_specs, scratch_k, scratch_v = [], [], []
    for g in range(C_GROUPS):
        halo = ATT_TQ * C_DILATIONS[g]
        per = tb // halo
        colblk = lambda s, g=g: (s * C_GROUPS + g) * C_HEADS
        cur = lambda s, g=g: pl.BlockSpec(
            (None, tb, C_HEAD), lambda bi, hi, ti, c=colblk(s): (bi, ti, c + hi))
        prv = lambda s, g=g, halo=halo, per=per: pl.BlockSpec(
            (None, halo, C_HEAD),
            lambda bi, hi, ti, c=colblk(s): (bi, jnp.maximum(ti * per - 1, 0), c + hi))
        in_specs += [cur(0), cur(1), prv(1), cur(2), prv(2)]
        scratch_k.append(pltpu.VMEM((tb + halo, C_HEAD), _F32))
        scratch_v.append(pltpu.VMEM((tb + halo, C_HEAD), _F32))
    return pl.pallas_call(
        _attn_seq_kernel,
        grid=(b, C_HEADS, t // tb),
        in_specs=in_specs,
        out_specs=pl.BlockSpec((None, tb, C_HEAD), lambda bi, hi, ti: (bi, ti, hi)),
        out_shape=jax.ShapeDtypeStruct((b, t, C_WIDTH), _F32),
        scratch_shapes=scratch_k + scratch_v + [pltpu.VMEM((C_GROUPS, tb, C_HEAD), _F32),
                                               pltpu.VMEM((C_GROUPS, tb, C_HEAD), _F32)],
        compiler_params=_cparams("parallel", "parallel", "parallel"),
        name="attn_seq",
    )(*([qkv] * (5 * C_GROUPS)))


def _attn_dec_kernel(q_ref, kn_ref, vn_ref, c0_ref, c1_ref, c2_ref, o_ref, *, t_new):
    nh = C_HEADS
    rq = q_ref.shape[1]
    outs, lses = [], []
    for gi, c_ref in enumerate((c0_ref, c1_ref, c2_ref)):
        d = C_DILATIONS[gi]
        L = c_ref.shape[1]
        q = q_ref[gi]
        kc = c_ref[0].reshape(L * nh, C_HEAD)
        vc = c_ref[1].reshape(L * nh, C_HEAD)
        s_c = _dot_nt(q, kc) * C_SCALE
        row = lax.broadcasted_iota(jnp.int32, (rq, L * nh), 0)
        colm = lax.broadcasted_iota(jnp.int32, (rq, L * nh), 1)
        dist = L + row // nh - colm // nh
        ok_c = (row % nh == colm % nh) & (dist % d == 0) & (dist <= (C_NKEYS - 1) * d)
        s_c = jnp.where(ok_c, s_c, -jnp.inf)
        m = jnp.max(s_c, axis=-1, keepdims=True)
        qtok = lax.broadcasted_iota(jnp.int32, (rq, 1), 0) // nh
        s_n, v_n = [], []
        for j in range(t_new):
            kj = jnp.concatenate([kn_ref[gi, j * nh:(j + 1) * nh, :]] * t_new, axis=0)
            v_n.append(jnp.concatenate([vn_ref[gi, j * nh:(j + 1) * nh, :]] * t_new, axis=0))
            dn = qtok - j
            ok = (dn >= 0) & (dn % d == 0) & (dn <= (C_NKEYS - 1) * d)
            col = jnp.sum(q * kj, axis=-1, keepdims=True) * C_SCALE
            s_n.append(jnp.where(ok, col, -jnp.inf))
            m = jnp.maximum(m, s_n[j])
        p_c = jnp.exp(s_c - m)
        l = jnp.sum(p_c, axis=-1, keepdims=True)
        o = _dot(p_c, vc)
        for j in range(t_new):
            p_j = jnp.exp(s_n[j] - m)
            l = l + p_j
            o = o + p_j * v_n[j]
        outs.append(o / l)
        lses.append(m + jnp.log(l))
    m = jnp.maximum(jnp.maximum(lses[0], lses[1]), lses[2])
    es = [jnp.exp(x - m) for x in lses]
    o_ref[...] = (es[0] * outs[0] + es[1] * outs[1] + es[2] * outs[2]) / (es[0] + es[1] + es[2])


def _attn_dec(qkv, caches, layer):
    b, t = qkv.shape[:2]
    rq = t * C_HEADS
    x = jnp.transpose(qkv, (2, 0, 3, 1, 4, 5)).reshape(3, b, C_GROUPS, rq, C_HEAD)
    new = lambda s: pl.BlockSpec((None, None, C_GROUPS, rq, C_HEAD), lambda bi: (s, bi, 0, 0, 0))
    cspec = lambda c: _once((None, 2, None) + c.shape[3:], lambda bi: (layer, 0, bi, 0, 0, 0))
    return pl.pallas_call(
        functools.partial(_attn_dec_kernel, t_new=t),
        grid=(b,),
        in_specs=[new(0), new(1), new(2)] + [cspec(c) for c in caches],
        out_specs=pl.BlockSpec((None, rq, C_HEAD), lambda bi: (bi, 0, 0)),
        out_shape=jax.ShapeDtypeStruct((b, rq, C_HEAD), _F32),
        compiler_params=_cparams("parallel"),
        name="attn_decode",
    )(x, x, x, *caches)


def _pad_rows(x, n, axis=1):
    pad = [(0, 0)] * x.ndim
    pad[axis] = (0, n - x.shape[axis])
    return jnp.pad(x, pad)


SEQ_TM = 1024
SEQ_TM_PREP = 256
SEQ_TM_MIX = 128
SEQ_CHUNK = 64
DEC_CHUNK = 32
FFN_TF = 256
SEQ_SCAN_BS = 2
DEC_SCAN_BS = 2


def _trunk(x, mods, shift_s, wkv_s, convb_s, ffn_s, kv_s, W, decode):
    bn, t, d = x.shape
    r = bn * t
    xf = x.reshape(r, d)
    if decode:
        tm = r
        bpb = 1
        modv = lambda l, k: jnp.repeat(mods[l, :, k], t, axis=0)[None]
    else:
        tm = min(SEQ_TM, t)
        bpb = t // tm
        modv = lambda l, k: mods[l, :, k][:, None, :]
    new_shift, new_wkv, new_convb, new_ffn = [], [], [], []
    new_kv = [[] for _ in C_WINDOWS]
    a3 = 3 * A_WIDTH
    for l in range(DEPTH):
        if l % 2 == 0:
            e = l // 2
            n_in = W['ab_w_in'].shape[2]
            p, h = _premix(xf, W['g_pre_mix'][l], modv(l, 0), modv(l, 1), (W['ab_w_in'], e),
                           n_in, tm, 512, bpb, emit_h=True)
            new_shift.append(h.reshape(bn, t, d)[:, -1])
            if decode:
                sp = shift_s[e]
                pa_st = _plain_mm(sp, (W['ab_w_in'], e), a3, 512)
                place = lambda s: jnp.zeros((bn, t, s.shape[-1]), _F32).at[:, 0].set(s).reshape(r, -1)
                outs = _rwkv_prep(h, p, W, e, tm, bpb, first=(place(sp), place(pa_st)), seg=t)
            else:
                tmp = min(SEQ_TM_PREP, t)
                outs = _rwkv_prep(h, p, W, e, tmp, t // tmp,
                                  state=(jnp.zeros((bn, 1, d), _F32), jnp.zeros((bn, 1, a3), _F32)))
            rr, lw, k2, vv, av, bv, gg, bon, u = [o.reshape(bn, t, -1) for o in outs]
            if decode:
                chunk = DEC_CHUNK
                scan_in = [_pad_rows(a_, chunk) for a_ in (rr, k2, vv, lw, av, bv)]
                z0 = _state_to_bd(wkv_s[e].astype(_F32))
            else:
                chunk = SEQ_CHUNK
                scan_in = [rr, k2, vv, lw, av, bv]
                z0 = jnp.zeros((bn, SCAN_NG, SCAN_GL, SCAN_GL), _F32)
            o, zf = _wkv_scan(*scan_in, z0, chunk, DEC_SCAN_BS if decode else SEQ_SCAN_BS)
            new_wkv.append(_bd_to_state(zf))
            cprev = convb_s[e] if decode else jnp.zeros((bn, B_CONV - 1, B_WIDTH), _F32)
            xx = jnp.concatenate([cprev, u], axis=1)
            new_convb.append(xx[:, t:])
            if decode:
                tp = _CONV_HALO
                mo = _mix_out(_pad_rows(o[:, :t], tp), _pad_rows(bon, tp), _pad_rows(gg, tp),
                              _pad_rows(xx, tp + _CONV_HALO), W, e, tp)[:, :t]
            else:
                mo = _mix_out(o, bon, gg, _pad_rows(xx, t + _CONV_HALO), W, e, SEQ_TM_MIX)
            xf = _post_mm(mo.reshape(r, -1), (W['ab_w_out'], e), xf, W['g_post_mix'][l], modv(l, 2),
                          tm, 512, bpb)
        else:
            oi = l // 2
            n_qkv = W['attn_w_qkv'].shape[2]
            qkv, = _premix(xf, W['g_pre_mix'][l], modv(l, 0), modv(l, 1), (W['attn_w_qkv'], oi),
                           n_qkv, tm, 512, bpb, emit_h=False)
            q5 = qkv.reshape(bn, t, 3, C_GROUPS, C_HEADS, C_HEAD)
            for gi in range(C_GROUPS):
                keep = min(C_WINDOWS[gi], t)
                new_kv[gi].append(jnp.stack([q5[:, t - keep:, 1, gi], q5[:, t - keep:, 2, gi]]))
            if decode:
                att = _attn_dec(q5, kv_s, oi).reshape(r, C_WIDTH)
            else:
                att = _attn_seq(qkv.reshape(bn, t, n_qkv)).reshape(r, C_WIDTH)
            xf = _post_mm(att, (W['attn_w_out'], oi), xf, W['g_post_mix'][l], modv(l, 2), tm, 512, bpb)
        if decode:
            st = ffn_s[l]
            f = st.shape[-1]
            p1 = jnp.zeros((bn, t, f), _F32).at[:, 0].set(st[:, 1]).reshape(r, f)
            p2 = jnp.zeros((bn, t, f), _F32).at[:, 0].set(st[:, 0]).at[:, 1].set(st[:, 1]).reshape(r, f)
            xf, gt = _ffn(xf, W['g_pre_ffn'][l], modv(l, 3), modv(l, 4), (W['ffn_w_gate'], l),
                          (W['ffn_w_up'], l), W['ffn_conv_w'][l], W['ffn_conv_b'][l], (W['ffn_w_down'], l),
                          W['g_post_ffn'][l], modv(l, 5), tm, FFN_TF, bpb, prev=(p1, p2), seg=t)
            new_ffn.append(gt.reshape(bn, t, f)[:, t - 2:])
        else:
            xf, gt = _ffn(xf, W['g_pre_ffn'][l], modv(l, 3), modv(l, 4), (W['ffn_w_gate'], l),
                          (W['ffn_w_up'], l), W['ffn_conv_w'][l], W['ffn_conv_b'][l], (W['ffn_w_down'], l),
                          W['g_post_ffn'][l], modv(l, 5), tm, FFN_TF, bpb)
            f = gt.shape[-1]
            new_ffn.append(gt.reshape(bn, bpb, SUBLANES, f)[:, -1, SUBLANES - 2:])
    return (xf.reshape(bn, t, d), jnp.stack(new_shift), jnp.stack(new_wkv), jnp.stack(new_convb),
            jnp.stack(new_ffn), jnp.stack(new_kv[0]), jnp.stack(new_kv[1]), jnp.stack(new_kv[2]))


def kernel(x_prompt, x_sample, state_shift, state_wkv, state_conv_b, state_ffn, cache_kv_w128, cache_kv_w512, cache_kv_w2048, c_prompt, c_sample, w_mod, b_mod, g_pre_mix, g_post_mix, g_pre_ffn, g_post_ffn, ab_w_in, a_mu_rkv, a_mu_wag, a_w0, a_w1, a_w2, a_a0, a_a1, a_a2, a_g1, a_g2, a_k_k, a_k_a, a_r_k, a_ln_w, a_ln_b, b_conv_w, b_conv_b, b_ln_w, b_ln_b, ab_w_out, attn_w_qkv, attn_w_out, ffn_w_gate, ffn_w_up, ffn_conv_w, ffn_conv_b, ffn_w_down):
    W = dict(w_mod=w_mod, b_mod=b_mod, g_pre_mix=g_pre_mix, g_post_mix=g_post_mix,
             g_pre_ffn=g_pre_ffn, g_post_ffn=g_post_ffn, ab_w_in=ab_w_in, a_mu_rkv=a_mu_rkv,
             a_mu_wag=a_mu_wag, a_w0=a_w0, a_w1=a_w1, a_w2=a_w2, a_a0=a_a0, a_a1=a_a1, a_a2=a_a2,
             a_g1=a_g1, a_g2=a_g2, a_k_k=a_k_k, a_k_a=a_k_a, a_r_k=a_r_k, a_ln_w=a_ln_w, a_ln_b=a_ln_b,
             b_conv_w=b_conv_w, b_conv_b=b_conv_b, b_ln_w=b_ln_w, b_ln_b=b_ln_b, ab_w_out=ab_w_out,
             attn_w_qkv=attn_w_qkv, attn_w_out=attn_w_out, ffn_w_gate=ffn_w_gate, ffn_w_up=ffn_w_up,
             ffn_conv_w=ffn_conv_w, ffn_conv_b=ffn_conv_b, ffn_w_down=ffn_w_down)
    nbp = x_prompt.shape[0]
    nbs = x_sample.shape[0]
    d = x_prompt.shape[-1]
    assert d == D_MODEL and w_mod.shape == (DEPTH, d, N_MOD * d)
    c_all = _pad_rows(jnp.concatenate([c_prompt, c_sample], axis=0), 16, axis=0)
    mods = _mods(c_all, w_mod, b_mod).reshape(DEPTH, 16, N_MOD, d)
    outs_p = _trunk(x_prompt, mods[:, :nbp], None, None, None, None, None, W, decode=False)
    outs_s = _trunk(x_sample, mods[:, nbp:nbp + nbs], state_shift, state_wkv, state_conv_b,
                    state_ffn, (cache_kv_w128, cache_kv_w512, cache_kv_w2048), W, decode=True)
    (y_p, p_shift, p_wkv, p_conv_b, p_ffn, p_kv0, p_kv1, p_kv2) = outs_p
    (y_s, s_shift, s_wkv, s_conv_b, s_ffn, s_kv0, s_kv1, s_kv2) = outs_s
    return (y_p, y_s, p_shift, p_wkv, p_conv_b, p_ffn, p_kv0, p_kv1, p_kv2,
            s_shift, s_wkv, s_conv_b, s_ffn, s_kv0, s_kv1, s_kv2)
```

```python
import functools

import jax
import jax.numpy as jnp
from jax import lax
from jax.experimental import pallas as pl
from jax.experimental.pallas import tpu as pltpu

_BF = jnp.bfloat16
_F32 = jnp.float32

D_MODEL = 2048
DEPTH = 4
N_MOD = 6
A_HEAD = 64
A_HEADS = 16
A_WIDTH = A_HEADS * A_HEAD
A_GN_EPS = 64e-5
B_WIDTH = 1024
B_CONV = 31
C_WINDOWS = (128, 512, 2048)
C_DILATIONS = (1, 4, 16)
C_GROUPS = 3
C_HEADS = 8
C_HEAD = 128
C_WIDTH = C_HEADS * C_HEAD
C_SCALE = C_HEAD ** -0.5
C_NKEYS = 129
D_FF = 5632
RMS_EPS = 1e-6
LN_EPS = 1e-5

SUBLANES = 8
LANES = 128
VMEM_LIMIT = 56 * 1024 * 1024
VMEM_LIMIT_FFN = 60 * 1024 * 1024

SCAN_G = 4
SCAN_GL = SCAN_G * A_HEAD
SCAN_NG = A_HEADS // SCAN_G


def _cparams(*sem, vmem=VMEM_LIMIT):
    return pltpu.CompilerParams(dimension_semantics=sem, vmem_limit_bytes=vmem)


def _dot(a, b):
    return jnp.dot(a.astype(_BF), b.astype(_BF), preferred_element_type=_F32)


def _dot_nt(a, b):
    return lax.dot_general(a.astype(_BF), b.astype(_BF), (((1,), (1,)), ((), ())),
                           preferred_element_type=_F32)


def _dot_tn(a, b):
    return lax.dot_general(a.astype(_BF), b.astype(_BF), (((0,), (0,)), ((), ())),
                           preferred_element_type=_F32)


def _split3(x):
    h1 = x.astype(_BF)
    r1 = x - h1.astype(_F32)
    h2 = r1.astype(_BF)
    h3 = (r1 - h2.astype(_F32)).astype(_BF)
    return h1, h2, h3


def _dot_exact_rhs(x, m_bf):
    h1, h2, h3 = _split3(x)
    return (jnp.dot(h1, m_bf, preferred_element_type=_F32)
            + jnp.dot(h2, m_bf, preferred_element_type=_F32)
            + jnp.dot(h3, m_bf, preferred_element_type=_F32))


def _split2(x):
    hi = x.astype(_BF)
    return hi, (x - hi.astype(_F32)).astype(_BF)


def _dot_hilo(lhs_list, rhs):
    his, los = zip(*[_split2(a) for a in lhs_list])
    rh, rl = _split2(rhs)
    y1 = jnp.dot(jnp.concatenate(his + los, axis=0), rh, preferred_element_type=_F32)
    y2 = jnp.dot(jnp.concatenate(his, axis=0), rl, preferred_element_type=_F32)
    tot = sum(a.shape[0] for a in lhs_list)
    outs, off = [], 0
    for a in lhs_list:
        m = a.shape[0]
        outs.append(y1[off:off + m] + y1[tot + off:tot + off + m] + y2[off:off + m])
        off += m
    return outs


def _sigmoid(x):
    return 1.0 / (1.0 + jnp.exp(-x))


def _silu(x):
    return x * _sigmoid(x)


def _rms(y, g):
    return y * lax.rsqrt(jnp.mean(y * y, axis=-1, keepdims=True) + RMS_EPS) * g


def _mod_kernel(c_ref, w_ref, b_ref, o_ref):
    c = c_ref[...]
    ch = c.astype(_BF)
    cl = (c - ch.astype(_F32)).astype(_BF)
    y = jnp.dot(jnp.concatenate([ch, cl], axis=0), w_ref[...].astype(_BF),
                preferred_element_type=_F32)
    n = c.shape[0]
    o_ref[...] = y[:n] + y[n:] + b_ref[...]


def _mods(c_all, w_mod, b_mod):
    depth, d, n = w_mod.shape
    rows = c_all.shape[0]
    tn = 1024
    return pl.pallas_call(
        _mod_kernel,
        grid=(depth, n // tn),
        in_specs=[pl.BlockSpec((rows, d), lambda l, j: (0, 0)),
                  pl.BlockSpec((None, d, tn), lambda l, j: (l, 0, j)),
                  pl.BlockSpec((None, 1, tn), lambda l, j: (l, 0, j))],
        out_specs=pl.BlockSpec((None, rows, tn), lambda l, j: (l, 0, j)),
        out_shape=jax.ShapeDtypeStruct((depth, rows, n), _F32),
        compiler_params=_cparams("parallel", "parallel"),
        name="mods",
    )(c_all, w_mod, b_mod.reshape(depth, 1, n))


def _premix_kernel(x_ref, g_ref, sh_ref, sc_ref, w_ref, o_ref, *rest):
    hs_ref = rest[-1]

    @pl.when(pl.program_id(1) == 0)
    def _():
        h = _rms(x_ref[...], g_ref[...]) * (1.0 + sc_ref[...]) + sh_ref[...]
        if len(rest) == 2:
            rest[0][...] = h
        hs_ref[...] = h.astype(_BF)

    o_ref[...] = jnp.dot(hs_ref[...], w_ref[...].astype(_BF), preferred_element_type=_F32)


def _mod_spec(mod, tm, bpb):
    ms = mod.shape[1]
    d = mod.shape[2]
    if ms == 1:
        return pl.BlockSpec((None, 1, d), lambda i, *_: (i // bpb, 0, 0))
    assert mod.shape[0] == 1 and ms == tm
    return pl.BlockSpec((None, ms, d), lambda i, *_: (0, 0, 0))


def _once(block_shape, index_map):
    return pl.BlockSpec(block_shape, index_map, pipeline_mode=pl.Buffered(1))


def _premix(x, g, shift, scale, w, n_out, tm, tn, bpb, emit_h):
    r, d = x.shape
    w, wl = w
    out_specs = [pl.BlockSpec((tm, tn), lambda i, j: (i, j))]
    out_shape = [jax.ShapeDtypeStruct((r, n_out), _F32)]
    if emit_h:
        out_specs.append(pl.BlockSpec((tm, d), lambda i, j: (i, 0)))
        out_shape.append(jax.ShapeDtypeStruct((r, d), _F32))
    return pl.pallas_call(
        _premix_kernel,
        grid=(r // tm, n_out // tn),
        in_specs=[_once((tm, d), lambda i, j: (i, 0)),
                  pl.BlockSpec((1, d), lambda i, j: (0, 0)),
                  _mod_spec(shift, tm, bpb), _mod_spec(scale, tm, bpb),
                  pl.BlockSpec((None, d, tn), lambda i, j: (wl, 0, j))],
        out_specs=out_specs,
        out_shape=out_shape,
        scratch_shapes=[pltpu.VMEM((tm, d), _BF)],
        compiler_params=_cparams("parallel", "arbitrary"),
        name="premix_matmul",
    )(x, g.reshape(1, d), shift, scale, w)


def _plain_mm_kernel(a_ref, w_ref, o_ref):
    o_ref[...] = _dot(a_ref[...], w_ref[...])


def _plain_mm(a, w, n_out, tn):
    m, k = a.shape
    w, wl = w
    return pl.pallas_call(
        _plain_mm_kernel,
        grid=(n_out // tn,),
        in_specs=[pl.BlockSpec((m, k), lambda j: (0, 0)),
                  pl.BlockSpec((None, k, tn), lambda j: (wl, 0, j))],
        out_specs=pl.BlockSpec((m, tn), lambda j: (0, j)),
        out_shape=jax.ShapeDtypeStruct((m, n_out), _F32),
        compiler_params=_cparams("parallel"),
        name="plain_matmul",
    )(a, w)


def _post_kernel(a_ref, w_ref, x_ref, g_ref, gm_ref, o_ref):
    k = pl.program_id(1)

    @pl.when(k == 0)
    def _():
        o_ref[...] = jnp.zeros_like(o_ref)

    o_ref[...] += _dot(a_ref[...], w_ref[...])

    @pl.when(k == pl.num_programs(1) - 1)
    def _():
        o_ref[...] = x_ref[...] + gm_ref[...] * _rms(o_ref[...], g_ref[...])


def _post_mm(a, w, x, g, gate, tm, tk, bpb):
    r, kdim = a.shape
    w, wl = w
    d = w.shape[2]
    return pl.pallas_call(
        _post_kernel,
        grid=(r // tm, kdim // tk),
        in_specs=[pl.BlockSpec((tm, tk), lambda i, k: (i, k)),
                  pl.BlockSpec((None, tk, d), lambda i, k: (wl, k, 0)),
                  _once((tm, d), lambda i, k: (i, 0)),
                  pl.BlockSpec((1, d), lambda i, k: (0, 0)),
                  _mod_spec(gate, tm, bpb)],
        out_specs=pl.BlockSpec((tm, d), lambda i, k: (i, 0)),
        out_shape=jax.ShapeDtypeStruct((r, d), _F32),
        compiler_params=_cparams("parallel", "arbitrary"),
        name="post_matmul",
    )(a, w, x, g.reshape(1, d), gate)


FFN_ROWS = 64


def _ffn_kernel(*refs, seq_mode, bpb, seg):
    if seq_mode:
        (x_ref, gpre_ref, sh_ref, sc_ref, wg_ref, wu_ref, cw_ref, cb_ref, wd_ref, gpost_ref,
         gm_ref, o_ref, gt_ref, hs_ref, carry_ref, gate_ref, up_ref, act_ref) = refs
    else:
        (x_ref, gpre_ref, sh_ref, sc_ref, wg_ref, wu_ref, cw_ref, cb_ref, wd_ref, gpost_ref,
         gm_ref, p1_ref, p2_ref, o_ref, gt_ref, hs_ref) = refs
    i = pl.program_id(0)
    j = pl.program_id(1)

    @pl.when(j == 0)
    def _():
        h = _rms(x_ref[...], gpre_ref[...]) * (1.0 + sc_ref[...]) + sh_ref[...]
        hs_ref[...] = h.astype(_BF)
        o_ref[...] = jnp.zeros_like(o_ref)

    hs = hs_ref[...]
    tm = hs.shape[0]
    cw = cw_ref[...]
    cb = cb_ref[...]
    if seq_mode:
        hdr = SUBLANES

        @pl.when(i % bpb == 0)
        def _():
            carry_ref[j] = jnp.zeros((hdr, gate_ref.shape[1]), _F32)

        gate_ref[:hdr] = carry_ref[j]
        gate_ref[hdr:] = jnp.dot(hs, wg_ref[...].astype(_BF), preferred_element_type=_F32)
        up_ref[...] = jnp.dot(hs, wu_ref[...].astype(_BF), preferred_element_type=_F32)
        tail = gate_ref[tm:]
        carry_ref[j] = tail
        gt_ref[...] = tail
        for c in range(tm // FFN_ROWS):
            r0 = hdr + c * FFN_ROWS
            gc = (cw[0:1] * gate_ref[r0 - 2:r0 - 2 + FFN_ROWS] + cw[1:2] * gate_ref[r0 - 1:r0 - 1 + FFN_ROWS]
                  + cw[2:3] * gate_ref[r0:r0 + FFN_ROWS] + cb)
            act_ref[c * FFN_ROWS:(c + 1) * FFN_ROWS] = (
                _silu(gc) * up_ref[c * FFN_ROWS:(c + 1) * FFN_ROWS]).astype(_BF)
        act = act_ref[...]
    else:
        gate = jnp.dot(hs, wg_ref[...].astype(_BF), preferred_element_type=_F32)
        up = jnp.dot(hs, wu_ref[...].astype(_BF), preferred_element_type=_F32)
        rs = lax.broadcasted_iota(jnp.int32, gate.shape, 0) % seg
        g1 = jnp.where(rs == 0, p1_ref[...], pltpu.roll(gate, 1, axis=0))
        g2 = jnp.where(rs < 2, p2_ref[...], pltpu.roll(gate, 2, axis=0))
        gt_ref[...] = gate
        act = (_silu(cw[0:1] * g2 + cw[1:2] * g1 + cw[2:3] * gate + cb) * up).astype(_BF)
    o_ref[...] += jnp.dot(act, wd_ref[...].astype(_BF), preferred_element_type=_F32)

    @pl.when(j == pl.num_programs(1) - 1)
    def _():
        o_ref[...] = x_ref[...] + gm_ref[...] * _rms(o_ref[...], gpost_ref[...])


def _ffn(x, gpre, shift, scale, wg, wu, cw, cb, wd, gpost, gate, tm, tf, bpb, prev=None, seg=1):
    r, d = x.shape
    (wg, lg), (wu, lu), (wd, ld) = wg, wu, wd
    f = wg.shape[2]
    nblk = r // tm
    seq_mode = prev is None
    gt_rows = SUBLANES if seq_mode else tm
    in_specs = [_once((tm, d), lambda i, j: (i, 0)),
                pl.BlockSpec((1, d), lambda i, j: (0, 0)),
                _mod_spec(shift, tm, bpb), _mod_spec(scale, tm, bpb),
                pl.BlockSpec((None, d, tf), lambda i, j: (lg, 0, j)),
                pl.BlockSpec((None, d, tf), lambda i, j: (lu, 0, j)),
                pl.BlockSpec((cw.shape[0], tf), lambda i, j: (0, j)),
                pl.BlockSpec((1, tf), lambda i, j: (0, j)),
                pl.BlockSpec((None, tf, d), lambda i, j: (ld, j, 0)),
                pl.BlockSpec((1, d), lambda i, j: (0, 0)),
                _mod_spec(gate, tm, bpb)]
    args = [x, gpre.reshape(1, d), shift, scale, wg, wu, cw, cb.reshape(1, f), wd,
            gpost.reshape(1, d), gate]
    scratch = [pltpu.VMEM((tm, d), _BF)]
    if seq_mode:
        assert tm % FFN_ROWS == 0
        scratch += [pltpu.VMEM((f // tf, SUBLANES, tf), _F32),
                    pltpu.VMEM((SUBLANES + tm, tf), _F32), pltpu.VMEM((tm, tf), _F32),
                    pltpu.VMEM((tm, tf), _BF)]
    else:
        in_specs += [pl.BlockSpec((tm, tf), lambda i, j: (i, j))] * 2
        args += list(prev)
    return pl.pallas_call(
        functools.partial(_ffn_kernel, seq_mode=seq_mode, bpb=bpb, seg=seg),
        grid=(nblk, f // tf),
        in_specs=in_specs,
        out_specs=[_once((tm, d), lambda i, j: (i, 0)),
                   pl.BlockSpec((None, gt_rows, tf), lambda i, j: (i, 0, j))],
        out_shape=[jax.ShapeDtypeStruct((r, d), _F32),
                   jax.ShapeDtypeStruct((nblk, gt_rows, f), _F32)],
        scratch_shapes=scratch,
        compiler_params=_cparams("arbitrary", "arbitrary", vmem=VMEM_LIMIT_FFN),
        name="conv_ffn",
    )(*args)


def _prev_rows(cur, first_row_of):
    return first_row_of(pltpu.roll(cur, 1, axis=0))


def _rwkv_prep_kernel(*refs, seq_mode, bpb, seg):
    if seq_mode:
        (h_ref, h8_ref, hst_ref, pa_ref, pa8_ref, past_ref, pb1_ref, pb2_ref) = refs[:8]
        rest = refs[8:]
    else:
        (h_ref, hf_ref, pa_ref, paf_ref, pb1_ref, pb2_ref) = refs[:6]
        rest = refs[6:]
    (murkv_ref, muwag_ref, w0_ref, w1_ref, w2_ref, a0_ref, a1_ref, a2_ref, g1_ref, g2_ref,
     kk_ref, ka_ref, rk_ref, ones_ref,
     r_out, lw_out, k_out, v_out, av_out, bv_out, g_out, bon_out, u_out) = rest
    i = pl.program_id(0)
    h = h_ref[...]
    pa = pa_ref[...]
    rows_h = lax.broadcasted_iota(jnp.int32, h.shape, 0)
    rows_p = lax.broadcasted_iota(jnp.int32, pa.shape, 0)
    hp = pltpu.roll(h, 1, axis=0)
    pp = pltpu.roll(pa, 1, axis=0)
    if seq_mode:
        first = i % bpb == 0
        h0 = jnp.where(first, hst_ref[...], h8_ref[SUBLANES - 1:SUBLANES])
        p0 = jnp.where(first, past_ref[...], pa8_ref[SUBLANES - 1:SUBLANES])
        hp = jnp.where(rows_h == 0, h0, hp)
        pp = jnp.where(rows_p == 0, p0, pp)
    else:
        hp = jnp.where(rows_h % seg == 0, hf_ref[...], hp)
        pp = jnp.where(rows_p % seg == 0, paf_ref[...], pp)
    delta = hp - h
    mu = muwag_ref[...]
    xw = h + delta * mu[0:1]
    xa = h + delta * mu[1:2]
    xg = h + delta * mu[2:3]
    zw = w0_ref[...] + _dot(jnp.tanh(_dot(xw, w1_ref[...])), w2_ref[...])
    w_log = -(jnp.maximum(-zw, 0.0) + jnp.log(1.0 + jnp.exp(-jnp.abs(zw)))) - 0.5
    lw_out[...] = -jnp.exp(w_log)
    a = _sigmoid(a0_ref[...] + _dot(_dot(xa, a1_ref[...]), a2_ref[...]))
    g_out[...] = _dot(_sigmoid(_dot(xg, g1_ref[...])), g2_ref[...])
    rkv = pa + (pp - pa) * murkv_ref[...]
    r = rkv[:, :A_WIDTH]
    k = rkv[:, A_WIDTH:2 * A_WIDTH]
    v = rkv[:, 2 * A_WIDTH:]
    ones = ones_ref[...]

    def segsum(x):
        parts = [_dot_exact_rhs(x[:, c * SCAN_GL:(c + 1) * SCAN_GL], ones)
                 for c in range(A_WIDTH // SCAN_GL)]
        return jnp.concatenate(parts, axis=1)

    kk = k * kk_ref[...]
    kk = kk * lax.rsqrt(jnp.maximum(segsum(kk * kk), 1e-24))
    k2 = k * (1.0 + (a - 1.0) * ka_ref[...])
    r_out[...] = r
    k_out[...] = k2
    v_out[...] = v
    av_out[...] = -kk
    bv_out[...] = kk * a
    bon_out[...] = segsum(r * k2 * rk_ref[...]) * v
    u_out[...] = pb1_ref[...] * _sigmoid(pb2_ref[...])


def _head_ones():
    idx = jnp.arange(SCAN_GL) // A_HEAD
    return (idx[:, None] == idx[None, :]).astype(_BF)


def _rwkv_prep(h, p, W, e, tm, bpb, state=None, first=None, seg=1):
    r, d = h.shape
    a3 = 3 * A_WIDTH
    seq_mode = first is None
    row = lambda n: pl.BlockSpec((tm, n), lambda i: (i, 0))
    full = lambda s: pl.BlockSpec(s, lambda i: (0,) * len(s))
    if seq_mode:
        t8 = tm // SUBLANES
        prev8 = lambda n: pl.BlockSpec((SUBLANES, n), lambda i: (jnp.maximum(i * t8 - 1, 0), 0))
        in_specs = [row(d), prev8(d), pl.BlockSpec((None, 1, d), lambda i: (i // bpb, 0, 0)),
                    row(a3), prev8(a3), pl.BlockSpec((None, 1, a3), lambda i: (i // bpb, 0, 0))]
        args = [h, h, state[0], p, p, state[1]]
    else:
        in_specs = [row(d), row(d), row(a3), row(a3)]
        args = [h, first[0], p, first[1]]
    in_specs += [pl.BlockSpec((tm, B_WIDTH), lambda i: (i, a3 // B_WIDTH)),
                 pl.BlockSpec((tm, B_WIDTH), lambda i: (i, a3 // B_WIDTH + 1))]
    args += [p, p]
    vec = lambda x: x.reshape(1, -1)
    lora = lambda n: -(-n // LANES) * LANES
    pad_c = lambda x: _pad_rows(x, lora(x.shape[1]), axis=1)
    pad_r = lambda x: _pad_rows(x, lora(x.shape[0]), axis=0)
    params = [vec(W['a_mu_rkv'][e]), W['a_mu_wag'][e], vec(W['a_w0'][e]), pad_c(W['a_w1'][e]),
              pad_r(W['a_w2'][e]), vec(W['a_a0'][e]), pad_c(W['a_a1'][e]), pad_r(W['a_a2'][e]),
              W['a_g1'][e],
              W['a_g2'][e], vec(W['a_k_k'][e]), vec(W['a_k_a'][e]), vec(W['a_r_k'][e]),
              _head_ones()]
    in_specs += [full(x.shape) for x in params]
    args += params
    out = jax.ShapeDtypeStruct((r, A_WIDTH), _F32)
    return pl.pallas_call(
        functools.partial(_rwkv_prep_kernel, seq_mode=seq_mode, bpb=bpb, seg=seg),
        grid=(r // tm,),
        in_specs=in_specs,
        out_specs=[row(A_WIDTH)] * 9,
        out_shape=[out] * 9,
        compiler_params=_cparams("parallel"),
        name="rwkv_prep",
    )(*args)


def _scan_groups(rcs, kcs, vcs, lws, avs, bvs, zbds, ltri):
    n = len(rcs)
    ix = range(n)
    c = rcs[0].shape[0]
    gc = SCAN_G * c
    cum = [_dot_exact_rhs_left(ltri, lws[i]) for i in ix]
    cl = [cum[i][c - 1:c] for i in ix]
    e_dn = [jnp.exp(-cum[i]) for i in ix]
    e_cl = [jnp.exp(cl[i] - cum[i]) for i in ix]
    rt = [rcs[i] * jnp.exp(cum[i]) for i in ix]
    at = [avs[i] * jnp.exp(cum[i] - lws[i]) for i in ix]
    kt = [kcs[i] * e_dn[i] for i in ix]
    bt = [bvs[i] * e_dn[i] for i in ix]
    kh = [kcs[i] * e_cl[i] for i in ix]
    bh = [bvs[i] * e_cl[i] for i in ix]

    row_h = lax.broadcasted_iota(jnp.int32, (gc, SCAN_GL), 0) // c
    lane_h = lax.broadcasted_iota(jnp.int32, (gc, SCAN_GL), 1) // A_HEAD
    hm_e = row_h == lane_h

    def expand(x):
        return jnp.where(hm_e, jnp.concatenate([x] * SCAN_G, axis=0), 0.0)

    bd_cc = (lax.broadcasted_iota(jnp.int32, (gc, gc), 0) // c
             == lax.broadcasted_iota(jnp.int32, (gc, gc), 1) // c)

    def bdiag(x):
        return jnp.where(bd_cc, jnp.concatenate([x] * SCAN_G, axis=0), 0.0)

    t_idx = lax.broadcasted_iota(jnp.int32, (c, gc), 0)
    j_idx = lax.broadcasted_iota(jnp.int32, (c, gc), 1) % c
    lo_s = j_idx < t_idx
    lo_i = j_idx <= t_idx
    amat = [_dot_nt(jnp.concatenate([at[i], rt[i]], axis=0),
                    jnp.concatenate([expand(kt[i]), expand(bt[i])], axis=0)) for i in ix]
    a_ak = [jnp.where(lo_s, amat[i][:c, :gc], 0.0) for i in ix]
    a_ab = [jnp.where(lo_s, amat[i][:c, gc:], 0.0) for i in ix]
    a_rk = [jnp.where(lo_i, amat[i][c:, :gc], 0.0) for i in ix]
    a_rb = [jnp.where(lo_i, amat[i][c:, gc:], 0.0) for i in ix]
    eye = jnp.where(j_idx == t_idx, 1.0, 0.0)
    trow = [eye + a_ab[i] for i in ix]
    pw = list(a_ab)
    nlev = c.bit_length() - 1
    for lev in range(nlev):
        for i in ix:
            lhs = ([trow[i]] if lev >= 1 else []) + ([pw[i]] if lev < nlev - 1 else [])
            res = _dot_hilo(lhs, bdiag(pw[i]))
            if lev >= 1:
                trow[i] = trow[i] + res[0]
            if lev < nlev - 1:
                pw[i] = res[-1]
    ve = [expand(vcs[i]) for i in ix]
    x1 = [_dot(a_ak[i], ve[i]) for i in ix]
    tu = [_dot(trow[i], jnp.concatenate([expand(x1[i]), expand(at[i])], axis=1)) for i in ix]
    u0 = [tu[i][:, :SCAN_GL] for i in ix]
    ap = [tu[i][:, SCAN_GL:] for i in ix]
    ru = [_dot(a_rb[i], jnp.concatenate([expand(ap[i]), expand(u0[i])], axis=1)) for i in ix]
    rp = [rt[i] + ru[i][:, :SCAN_GL] for i in ix]
    o0 = [_dot(a_rk[i], ve[i]) + ru[i][:, SCAN_GL:] for i in ix]
    sz = [_dot(jnp.concatenate([rp[i], ap[i]], axis=0), zbds[i]) for i in ix]
    o = [sz[i][:c] + o0[i] for i in ix]
    u = [sz[i][c:] + u0[i] for i in ix]
    dmask = (lax.broadcasted_iota(jnp.int32, (A_HEAD, SCAN_GL), 1) % A_HEAD
             == lax.broadcasted_iota(jnp.int32, (A_HEAD, SCAN_GL), 0))
    bd_ll = (lax.broadcasted_iota(jnp.int32, (SCAN_GL, SCAN_GL), 0) // A_HEAD
             == lax.broadcasted_iota(jnp.int32, (SCAN_GL, SCAN_GL), 1) // A_HEAD)
    znew = []
    for i in ix:
        zc = zbds[i][0:A_HEAD]
        for h in range(1, SCAN_G):
            zc = zc + zbds[i][h * A_HEAD:(h + 1) * A_HEAD]
        dg = jnp.where(dmask, jnp.exp(cl[i]), 0.0)
        xs = jnp.concatenate([kh[i], bh[i], dg], axis=0)
        ys = jnp.concatenate([vcs[i], u[i], zc], axis=0)
        znew.append(jnp.where(bd_ll, _dot_tn(xs, ys), 0.0))
    return o, znew


def _dot_exact_rhs_left(m_bf, x):
    h1, h2, h3 = _split3(x)
    return (jnp.dot(m_bf, h1, preferred_element_type=_F32)
            + jnp.dot(m_bf, h2, preferred_element_type=_F32)
            + jnp.dot(m_bf, h3, preferred_element_type=_F32))


def _scan_kernel(r_ref, k_ref, v_ref, lw_ref, av_ref, bv_ref, z0_ref, o_ref, zout_ref, z_ref):
    t = pl.program_id(1)

    @pl.when(t == 0)
    def _():
        z_ref[...] = z0_ref[...]

    bs, c = r_ref.shape[:2]
    ltri = (lax.broadcasted_iota(jnp.int32, (c, c), 0)
            >= lax.broadcasted_iota(jnp.int32, (c, c), 1)).astype(_BF)
    chains = [(bi, gi) for bi in range(bs) for gi in range(SCAN_NG)]
    lanes = lambda gi: slice(gi * SCAN_GL, (gi + 1) * SCAN_GL)
    tok = lambda ref: [ref[bi, :, lanes(gi)] for bi, gi in chains]
    o, znew = _scan_groups(tok(r_ref), tok(k_ref), tok(v_ref), tok(lw_ref), tok(av_ref),
                           tok(bv_ref), [z_ref[bi, gi] for bi, gi in chains], ltri)
    for n, (bi, gi) in enumerate(chains):
        o_ref[bi, :, lanes(gi)] = o[n]
        z_ref[bi, gi] = znew[n]

    @pl.when(t == pl.num_programs(1) - 1)
    def _():
        zout_ref[...] = z_ref[...]


def _wkv_scan(r, k, v, lw, av, bv, z0, chunk, bs):
    b, t, _ = r.shape
    tok = pl.BlockSpec((bs, chunk, A_WIDTH), lambda bi, ti: (bi, ti, 0))
    zspec = pl.BlockSpec((bs, SCAN_NG, SCAN_GL, SCAN_GL), lambda bi, ti: (bi, 0, 0, 0))
    return pl.pallas_call(
        _scan_kernel,
        grid=(b // bs, t // chunk),
        in_specs=[tok] * 6 + [zspec],
        out_specs=[tok, zspec],
        out_shape=[jax.ShapeDtypeStruct((b, t, A_WIDTH), _F32),
                   jax.ShapeDtypeStruct((b, SCAN_NG, SCAN_GL, SCAN_GL), _F32)],
        scratch_shapes=[pltpu.VMEM((bs, SCAN_NG, SCAN_GL, SCAN_GL), _F32)],
        compiler_params=_cparams("parallel", "arbitrary"),
        name="wkv7_scan",
    )(r, k, v, lw, av, bv, z0)


def _state_to_bd(s):
    b = s.shape[0]
    st = jnp.swapaxes(s, -1, -2).reshape(b, SCAN_NG, SCAN_G, A_HEAD, A_HEAD)
    eye = jnp.eye(SCAN_G, dtype=s.dtype)
    z = st[:, :, :, :, None, :] * eye[None, None, :, None, :, None]
    return z.reshape(b, SCAN_NG, SCAN_GL, SCAN_GL)


def _bd_to_state(z):
    b = z.shape[0]
    z = z.reshape(b, SCAN_NG, SCAN_G, A_HEAD, SCAN_G, A_HEAD)
    idx = jnp.arange(SCAN_G)
    st = z[:, :, idx, :, idx, :]
    st = jnp.moveaxis(st, 0, 2).reshape(b, A_HEADS, A_HEAD, A_HEAD)
    return jnp.swapaxes(st, -1, -2)


def _mix_out_kernel(o_ref, bon_ref, g_ref, lnw_ref, lnb_ref, xm_ref, xh_ref, cw_ref, cb_ref,
                    blw_ref, blb_ref, ones_ref, out_ref, win_ref, sh_ref):
    tm = o_ref.shape[0]
    ones = ones_ref[...]

    def segmean(x):
        parts = [_dot_exact_rhs(x[:, c * SCAN_GL:(c + 1) * SCAN_GL], ones)
                 for c in range(A_WIDTH // SCAN_GL)]
        return jnp.concatenate(parts, axis=1) * (1.0 / A_HEAD)

    o = o_ref[...]
    oc = o - segmean(o)
    var = segmean(oc * oc)
    on = oc * lax.rsqrt(var + A_GN_EPS) * lnw_ref[...] + lnb_ref[...]
    out_ref[:, :A_WIDTH] = (on + bon_ref[...]) * g_ref[...]

    win_ref[:tm] = xm_ref[...]
    win_ref[tm:] = xh_ref[...]
    span = tm + (B_CONV - 1) // SUBLANES * SUBLANES
    for s in range(1, SUBLANES):
        sh_ref[s - 1] = win_ref[s:s + span, :]
    cols = []
    for c in range(B_WIDTH // LANES):
        cs = slice(c * LANES, (c + 1) * LANES)
        acc = jnp.zeros((tm, LANES), _F32) + cb_ref[:, cs]
        for j in range(B_CONV):
            a8, s = j // SUBLANES * SUBLANES, j % SUBLANES
            tap = win_ref[a8:a8 + tm, cs] if s == 0 else sh_ref[s - 1, a8:a8 + tm, cs]
            acc = acc + cw_ref[j:j + 1, cs] * tap
        cols.append(acc)
    ub = jnp.concatenate(cols, axis=1)
    uc = ub - jnp.mean(ub, axis=-1, keepdims=True)
    uv = jnp.mean(uc * uc, axis=-1, keepdims=True)
    out_ref[:, A_WIDTH:] = _silu(uc * lax.rsqrt(uv + LN_EPS) * blw_ref[...] + blb_ref[...])


_CONV_HALO = 32


def _mix_out(o, bon, g, xx, W, e, tm):
    b, t, _ = o.shape
    tok = pl.BlockSpec((None, tm, A_WIDTH), lambda bi, ti: (bi, ti, 0))
    full = lambda s: pl.BlockSpec(s, lambda bi, ti: (0,) * len(s))
    hb = tm // _CONV_HALO
    vec = lambda x: x.reshape(1, -1)
    params = [vec(W['a_ln_w'][e]), vec(W['a_ln_b'][e])]
    conv = [W['b_conv_w'][e], vec(W['b_conv_b'][e]), vec(W['b_ln_w'][e]), vec(W['b_ln_b'][e]),
            _head_ones()]
    return pl.pallas_call(
        _mix_out_kernel,
        grid=(b, t // tm),
        in_specs=[tok, tok, tok] + [full(x.shape) for x in params]
        + [pl.BlockSpec((None, tm, B_WIDTH), lambda bi, ti: (bi, ti, 0)),
           pl.BlockSpec((None, _CONV_HALO, B_WIDTH), lambda bi, ti: (bi, (ti + 1) * hb, 0))]
        + [full(x.shape) for x in conv],
        out_specs=pl.BlockSpec((None, tm, A_WIDTH + B_WIDTH), lambda bi, ti: (bi, ti, 0)),
        out_shape=jax.ShapeDtypeStruct((b, t, A_WIDTH + B_WIDTH), _F32),
        scratch_shapes=[pltpu.VMEM((tm + _CONV_HALO, B_WIDTH), _F32),
                        pltpu.VMEM((SUBLANES - 1, tm + (B_CONV - 1) // SUBLANES * SUBLANES, B_WIDTH),
                                   _F32)],
        compiler_params=_cparams("parallel", "parallel"),
        name="mix_out",
    )(o, bon, g, *params, xx, xx, *conv)


ATT_UNROLL = 4
ATT_TQ = 128
ATT_TB = ATT_TQ * max(C_DILATIONS)


def _attn_seq_kernel(*refs):
    ng = C_GROUPS
    ins = [refs[5 * g:5 * g + 5] for g in range(ng)]
    o_ref = refs[5 * ng]
    kw_refs = refs[5 * ng + 1:5 * ng + 1 + ng]
    vw_refs = refs[5 * ng + 1 + ng:5 * ng + 1 + 2 * ng]
    og_ref, lg_ref = refs[5 * ng + 1 + 2 * ng:]
    ti = pl.program_id(2)
    tq = ATT_TQ
    row = lax.broadcasted_iota(jnp.int32, (tq, 2 * tq), 0)
    col = lax.broadcasted_iota(jnp.int32, (tq, 2 * tq), 1)
    diff = tq + row - col
    band = (diff >= 0) & (diff <= C_NKEYS - 1)
    for g in range(ng):
        d = C_DILATIONS[g]
        q_ref, k_ref, kp_ref, v_ref, vp_ref = ins[g]
        kw, vw = kw_refs[g], vw_refs[g]
        halo = tq * d
        kw[:halo] = kp_ref[...]
        kw[halo:] = k_ref[...]
        vw[:halo] = vp_ref[...]
        vw[halo:] = v_ref[...]
        nsb = ATT_TB // halo

        def tiles(it, carry, d=d, g=g, kw=kw, vw=vw, q_ref=q_ref):
            us = range(ATT_UNROLL)
            ns = [it * ATT_UNROLL + u for u in us]
            sbs = [n // d for n in ns]
            starts = [n % d + (n // d) * (tq * d) for n in ns]
            if d == 1:
                starts = [pl.multiple_of(st, tq) for st in starts]
                rows = lambda st, n_rows: pl.ds(st, n_rows)
            else:
                rows = lambda st, n_rows: pl.ds(st, n_rows, stride=d)
            ss = [_dot_nt(q_ref[rows(starts[u], tq), :], kw[rows(starts[u], 2 * tq), :]) * C_SCALE
                  for u in us]
            ss = [jnp.where(band & ((col >= tq) | (sbs[u] > 0) | (ti > 0)), ss[u], -jnp.inf)
                  for u in us]
            ms = [jnp.max(ss[u], axis=-1, keepdims=True) for u in us]
            ps = [jnp.exp(ss[u] - ms[u]) for u in us]
            ls = [jnp.sum(ps[u], axis=-1, keepdims=True) for u in us]
            os_ = [_dot(ps[u], vw[rows(starts[u], 2 * tq), :]) / ls[u] for u in us]
            for u in us:
                og_ref[g, rows(starts[u], tq), :] = os_[u]
                lg_ref[g, rows(starts[u], tq), :] = jnp.broadcast_to(ms[u] + jnp.log(ls[u]),
                                                                     (tq, C_HEAD))
            return carry

        lax.fori_loop(0, d * nsb // ATT_UNROLL, tiles, 0)
    a0, a1, a2 = lg_ref[0], lg_ref[1], lg_ref[2]
    m = jnp.maximum(jnp.maximum(a0, a1), a2)
    e0, e1, e2 = jnp.exp(a0 - m), jnp.exp(a1 - m), jnp.exp(a2 - m)
    o_ref[...] = (e0 * og_ref[0] + e1 * og_ref[1] + e2 * og_ref[2]) / (e0 + e1 + e2)


def _attn_seq(qkv):
    b, t, n = qkv.shape
    tb = ATT_TB
    in_specs, scratch_k, scratch_v = [], [], []
    for g in range(C_GROUPS):
        halo = ATT_TQ * C_DILATIONS[g]
        per = tb // halo
        colblk = lambda s, g=g: (s * C_GROUPS + g) * C_HEADS
        cur = lambda s, g=g: pl.BlockSpec(
            (None, tb, C_HEAD), lambda bi, hi, ti, c=colblk(s): (bi, ti, c + hi))
        prv = lambda s, g=g, halo=halo, per=per: pl.BlockSpec(
            (None, halo, C_HEAD),
            lambda bi, hi, ti, c=colblk(s): (bi, jnp.maximum(ti * per - 1, 0), c + hi))
        in_specs += [cur(0), cur(1), prv(1), cur(2), prv(2)]
        scratch_k.append(pltpu.VMEM((tb + halo, C_HEAD), _F32))
        scratch_v.append(pltpu.VMEM((tb + halo, C_HEAD), _F32))
    return pl.pallas_call(
        _attn_seq_kernel,
        grid=(b, C_HEADS, t // tb),
        in_specs=in_specs,
        out_specs=pl.BlockSpec((None, tb, C_HEAD), lambda bi, hi, ti: (bi, ti, hi)),
        out_shape=jax.ShapeDtypeStruct((b, t, C_WIDTH), _F32),
        scratch_shapes=scratch_k + scratch_v + [pltpu.VMEM((C_GROUPS, tb, C_HEAD), _F32),
                                               pltpu.VMEM((C_GROUPS, tb, C_HEAD), _F32)],
        compiler_params=_cparams("parallel", "parallel", "parallel"),
        name="attn_seq",
    )(*([qkv] * (5 * C_GROUPS)))


def _attn_dec_kernel(q_ref, kn_ref, vn_ref, c0_ref, c1_ref, c2_ref, o_ref, *, t_new):
    nh = C_HEADS
    rq = q_ref.shape[1]
    outs, lses = [], []
    for gi, c_ref in enumerate((c0_ref, c1_ref, c2_ref)):
        d = C_DILATIONS[gi]
        L = c_ref.shape[1]
        q = q_ref[gi]
        kc = c_ref[0].reshape(L * nh, C_HEAD)
        vc = c_ref[1].reshape(L * nh, C_HEAD)
        s_c = _dot_nt(q, kc) * C_SCALE
        row = lax.broadcasted_iota(jnp.int32, (rq, L * nh), 0)
        colm = lax.broadcasted_iota(jnp.int32, (rq, L * nh), 1)
        dist = L + row // nh - colm // nh
        ok_c = (row % nh == colm % nh) & (dist % d == 0) & (dist <= (C_NKEYS - 1) * d)
        s_c = jnp.where(ok_c, s_c, -jnp.inf)
        m = jnp.max(s_c, axis=-1, keepdims=True)
        qtok = lax.broadcasted_iota(jnp.int32, (rq, 1), 0) // nh
        s_n, v_n = [], []
        for j in range(t_new):
            kj = jnp.concatenate([kn_ref[gi, j * nh:(j + 1) * nh, :]] * t_new, axis=0)
            v_n.append(jnp.concatenate([vn_ref[gi, j * nh:(j + 1) * nh, :]] * t_new, axis=0))
            dn = qtok - j
            ok = (dn >= 0) & (dn % d == 0) & (dn <= (C_NKEYS - 1) * d)
            col = jnp.sum(q * kj, axis=-1, keepdims=True) * C_SCALE
            s_n.append(jnp.where(ok, col, -jnp.inf))
            m = jnp.maximum(m, s_n[j])
        p_c = jnp.exp(s_c - m)
        l = jnp.sum(p_c, axis=-1, keepdims=True)
        o = _dot(p_c, vc)
        for j in range(t_new):
            p_j = jnp.exp(s_n[j] - m)
            l = l + p_j
            o = o + p_j * v_n[j]
        outs.append(o / l)
        lses.append(m + jnp.log(l))
    m = jnp.maximum(jnp.maximum(lses[0], lses[1]), lses[2])
    es = [jnp.exp(x - m) for x in lses]
    o_ref[...] = (es[0] * outs[0] + es[1] * outs[1] + es[2] * outs[2]) / (es[0] + es[1] + es[2])


def _attn_dec(qkv, caches, layer):
    b, t = qkv.shape[:2]
    rq = t * C_HEADS
    x = jnp.transpose(qkv, (2, 0, 3, 1, 4, 5)).reshape(3, b, C_GROUPS, rq, C_HEAD)
    new = lambda s: pl.BlockSpec((None, None, C_GROUPS, rq, C_HEAD), lambda bi: (s, bi, 0, 0, 0))
    cspec = lambda c: _once((None, 2, None) + c.shape[3:], lambda bi: (layer, 0, bi, 0, 0, 0))
    return pl.pallas_call(
        functools.partial(_attn_dec_kernel, t_new=t),
        grid=(b,),
        in_specs=[new(0), new(1), new(2)] + [cspec(c) for c in caches],
        out_specs=pl.BlockSpec((None, rq, C_HEAD), lambda bi: (bi, 0, 0)),
        out_shape=jax.ShapeDtypeStruct((b, rq, C_HEAD), _F32),
        compiler_params=_cparams("parallel"),
        name="attn_decode",
    )(x, x, x, *caches)


def _pad_rows(x, n, axis=1):
    pad = [(0, 0)] * x.ndim
    pad[axis] = (0, n - x.shape[axis])
    return jnp.pad(x, pad)


SEQ_TM = 1024
SEQ_TM_PREP = 256
SEQ_TM_MIX = 128
SEQ_CHUNK = 64
DEC_CHUNK = 32
FFN_TF = 512
QKV_TN = 1024
SEQ_SCAN_BS = 2
DEC_SCAN_BS = 2


def _trunk(x, mods, shift_s, wkv_s, convb_s, ffn_s, kv_s, W, decode):
    bn, t, d = x.shape
    r = bn * t
    xf = x.reshape(r, d)
    if decode:
        tm = r
        bpb = 1
        modv = lambda l, k: jnp.repeat(mods[l, :, k], t, axis=0)[None]
    else:
        tm = min(SEQ_TM, t)
        bpb = t // tm
        modv = lambda l, k: mods[l, :, k][:, None, :]
    new_shift, new_wkv, new_convb, new_ffn = [], [], [], []
    new_kv = [[] for _ in C_WINDOWS]
    a3 = 3 * A_WIDTH
    for l in range(DEPTH):
        if l % 2 == 0:
            e = l // 2
            n_in = W['ab_w_in'].shape[2]
            p, h = _premix(xf, W['g_pre_mix'][l], modv(l, 0), modv(l, 1), (W['ab_w_in'], e),
                           n_in, tm, 512, bpb, emit_h=True)
            new_shift.append(h.reshape(bn, t, d)[:, -1])
            if decode:
                sp = shift_s[e]
                pa_st = _plain_mm(sp, (W['ab_w_in'], e), a3, 512)
                place = lambda s: jnp.zeros((bn, t, s.shape[-1]), _F32).at[:, 0].set(s).reshape(r, -1)
                outs = _rwkv_prep(h, p, W, e, tm, bpb, first=(place(sp), place(pa_st)), seg=t)
            else:
                tmp = min(SEQ_TM_PREP, t)
                outs = _rwkv_prep(h, p, W, e, tmp, t // tmp,
                                  state=(jnp.zeros((bn, 1, d), _F32), jnp.zeros((bn, 1, a3), _F32)))
            rr, lw, k2, vv, av, bv, gg, bon, u = [o.reshape(bn, t, -1) for o in outs]
            if decode:
                chunk = DEC_CHUNK
                scan_in = [_pad_rows(a_, chunk) for a_ in (rr, k2, vv, lw, av, bv)]
                z0 = _state_to_bd(wkv_s[e].astype(_F32))
            else:
                chunk = SEQ_CHUNK
                scan_in = [rr, k2, vv, lw, av, bv]
                z0 = jnp.zeros((bn, SCAN_NG, SCAN_GL, SCAN_GL), _F32)
            o, zf = _wkv_scan(*scan_in, z0, chunk, DEC_SCAN_BS if decode else SEQ_SCAN_BS)
            new_wkv.append(_bd_to_state(zf))
            cprev = convb_s[e] if decode else jnp.zeros((bn, B_CONV - 1, B_WIDTH), _F32)
            xx = jnp.concatenate([cprev, u], axis=1)
            new_convb.append(xx[:, t:])
            if decode:
                tp = _CONV_HALO
                mo = _mix_out(_pad_rows(o[:, :t], tp), _pad_rows(bon, tp), _pad_rows(gg, tp),
                              _pad_rows(xx, tp + _CONV_HALO), W, e, tp)[:, :t]
            else:
                mo = _mix_out(o, bon, gg, _pad_rows(xx, t + _CONV_HALO), W, e, SEQ_TM_MIX)
            xf = _post_mm(mo.reshape(r, -1), (W['ab_w_out'], e), xf, W['g_post_mix'][l], modv(l, 2),
                          tm, 512, bpb)
        else:
            oi = l // 2
            n_qkv = W['attn_w_qkv'].shape[2]
            qkv, = _premix(xf, W['g_pre_mix'][l], modv(l, 0), modv(l, 1), (W['attn_w_qkv'], oi),
                           n_qkv, tm, QKV_TN, bpb, emit_h=False)
            q5 = qkv.reshape(bn, t, 3, C_GROUPS, C_HEADS, C_HEAD)
            for gi in range(C_GROUPS):
                keep = min(C_WINDOWS[gi], t)
                new_kv[gi].append(jnp.stack([q5[:, t - keep:, 1, gi], q5[:, t - keep:, 2, gi]]))
            if decode:
                att = _attn_dec(q5, kv_s, oi).reshape(r, C_WIDTH)
            else:
                att = _attn_seq(qkv.reshape(bn, t, n_qkv)).reshape(r, C_WIDTH)
            xf = _post_mm(att, (W['attn_w_out'], oi), xf, W['g_post_mix'][l], modv(l, 2), tm, 512, bpb)
        if decode:
            st = ffn_s[l]
            f = st.shape[-1]
            p1 = jnp.zeros((bn, t, f), _F32).at[:, 0].set(st[:, 1]).reshape(r, f)
            p2 = jnp.zeros((bn, t, f), _F32).at[:, 0].set(st[:, 0]).at[:, 1].set(st[:, 1]).reshape(r, f)
            xf, gt = _ffn(xf, W['g_pre_ffn'][l], modv(l, 3), modv(l, 4), (W['ffn_w_gate'], l),
                          (W['ffn_w_up'], l), W['ffn_conv_w'][l], W['ffn_conv_b'][l], (W['ffn_w_down'], l),
                          W['g_post_ffn'][l], modv(l, 5), tm, FFN_TF, bpb, prev=(p1, p2), seg=t)
            new_ffn.append(gt.reshape(bn, t, f)[:, t - 2:])
        else:
            xf, gt = _ffn(xf, W['g_pre_ffn'][l], modv(l, 3), modv(l, 4), (W['ffn_w_gate'], l),
                          (W['ffn_w_up'], l), W['ffn_conv_w'][l], W['ffn_conv_b'][l], (W['ffn_w_down'], l),
                          W['g_post_ffn'][l], modv(l, 5), tm, FFN_TF, bpb)
            f = gt.shape[-1]
            new_ffn.append(gt.reshape(bn, bpb, SUBLANES, f)[:, -1, SUBLANES - 2:])
    return (xf.reshape(bn, t, d), jnp.stack(new_shift), jnp.stack(new_wkv), jnp.stack(new_convb),
            jnp.stack(new_ffn), jnp.stack(new_kv[0]), jnp.stack(new_kv[1]), jnp.stack(new_kv[2]))


def kernel(x_prompt, x_sample, state_shift, state_wkv, state_conv_b, state_ffn, cache_kv_w128, cache_kv_w512, cache_kv_w2048, c_prompt, c_sample, w_mod, b_mod, g_pre_mix, g_post_mix, g_pre_ffn, g_post_ffn, ab_w_in, a_mu_rkv, a_mu_wag, a_w0, a_w1, a_w2, a_a0, a_a1, a_a2, a_g1, a_g2, a_k_k, a_k_a, a_r_k, a_ln_w, a_ln_b, b_conv_w, b_conv_b, b_ln_w, b_ln_b, ab_w_out, attn_w_qkv, attn_w_out, ffn_w_gate, ffn_w_up, ffn_conv_w, ffn_conv_b, ffn_w_down):
    W = dict(w_mod=w_mod, b_mod=b_mod, g_pre_mix=g_pre_mix, g_post_mix=g_post_mix,
             g_pre_ffn=g_pre_ffn, g_post_ffn=g_post_ffn, ab_w_in=ab_w_in, a_mu_rkv=a_mu_rkv,
             a_mu_wag=a_mu_wag, a_w0=a_w0, a_w1=a_w1, a_w2=a_w2, a_a0=a_a0, a_a1=a_a1, a_a2=a_a2,
             a_g1=a_g1, a_g2=a_g2, a_k_k=a_k_k, a_k_a=a_k_a, a_r_k=a_r_k, a_ln_w=a_ln_w, a_ln_b=a_ln_b,
             b_conv_w=b_conv_w, b_conv_b=b_conv_b, b_ln_w=b_ln_w, b_ln_b=b_ln_b, ab_w_out=ab_w_out,
             attn_w_qkv=attn_w_qkv, attn_w_out=attn_w_out, ffn_w_gate=ffn_w_gate, ffn_w_up=ffn_w_up,
             ffn_conv_w=ffn_conv_w, ffn_conv_b=ffn_conv_b, ffn_w_down=ffn_w_down)
    nbp = x_prompt.shape[0]
    nbs = x_sample.shape[0]
    d = x_prompt.shape[-1]
    assert d == D_MODEL and w_mod.shape == (DEPTH, d, N_MOD * d)
    c_all = _pad_rows(jnp.concatenate([c_prompt, c_sample], axis=0), 16, axis=0)
    mods = _mods(c_all, w_mod, b_mod).reshape(DEPTH, 16, N_MOD, d)
    outs_p = _trunk(x_prompt, mods[:, :nbp], None, None, None, None, None, W, decode=False)
    outs_s = _trunk(x_sample, mods[:, nbp:nbp + nbs], state_shift, state_wkv, state_conv_b,
                    state_ffn, (cache_kv_w128, cache_kv_w512, cache_kv_w2048), W, decode=True)
    (y_p, p_shift, p_wkv, p_conv_b, p_ffn, p_kv0, p_kv1, p_kv2) = outs_p
    (y_s, s_shift, s_wkv, s_conv_b, s_ffn, s_kv0, s_kv1, s_kv2) = outs_s
    return (y_p, y_s, p_shift, p_wkv, p_conv_b, p_ffn, p_kv0, p_kv1, p_kv2,
            s_shift, s_wkv, s_conv_b, s_ffn, s_kv0, s_kv1, s_kv2)
```

```python
import functools

import jax
import jax.numpy as jnp
from jax import lax
from jax.experimental import pallas as pl
from jax.experimental.pallas import tpu as pltpu

_BF = jnp.bfloat16
_F32 = jnp.float32

D_MODEL = 2048
DEPTH = 4
N_MOD = 6
A_HEAD = 64
A_HEADS = 16
A_WIDTH = A_HEADS * A_HEAD
A_GN_EPS = 64e-5
B_WIDTH = 1024
B_CONV = 31
C_WINDOWS = (128, 512, 2048)
C_DILATIONS = (1, 4, 16)
C_GROUPS = 3
C_HEADS = 8
C_HEAD = 128
C_WIDTH = C_HEADS * C_HEAD
C_SCALE = C_HEAD ** -0.5
C_NKEYS = 129
D_FF = 5632
RMS_EPS = 1e-6
LN_EPS = 1e-5

SUBLANES = 8
LANES = 128
VMEM_LIMIT = 56 * 1024 * 1024
VMEM_LIMIT_FFN = 60 * 1024 * 1024

SCAN_G = 4
SCAN_GL = SCAN_G * A_HEAD
SCAN_NG = A_HEADS // SCAN_G


def _cparams(*sem, vmem=VMEM_LIMIT):
    return pltpu.CompilerParams(dimension_semantics=sem, vmem_limit_bytes=vmem)


def _dot(a, b):
    return jnp.dot(a.astype(_BF), b.astype(_BF), preferred_element_type=_F32)


def _dot_nt(a, b):
    return lax.dot_general(a.astype(_BF), b.astype(_BF), (((1,), (1,)), ((), ())),
                           preferred_element_type=_F32)


def _dot_tn(a, b):
    return lax.dot_general(a.astype(_BF), b.astype(_BF), (((0,), (0,)), ((), ())),
                           preferred_element_type=_F32)


def _split3(x):
    h1 = x.astype(_BF)
    r1 = x - h1.astype(_F32)
    h2 = r1.astype(_BF)
    h3 = (r1 - h2.astype(_F32)).astype(_BF)
    return h1, h2, h3


def _dot_exact_rhs(x, m_bf):
    h1, h2, h3 = _split3(x)
    return (jnp.dot(h1, m_bf, preferred_element_type=_F32)
            + jnp.dot(h2, m_bf, preferred_element_type=_F32)
            + jnp.dot(h3, m_bf, preferred_element_type=_F32))


def _split2(x):
    hi = x.astype(_BF)
    return hi, (x - hi.astype(_F32)).astype(_BF)


def _dot_hilo(lhs_list, rhs):
    his, los = zip(*[_split2(a) for a in lhs_list])
    rh, rl = _split2(rhs)
    y1 = jnp.dot(jnp.concatenate(his + los, axis=0), rh, preferred_element_type=_F32)
    y2 = jnp.dot(jnp.concatenate(his, axis=0), rl, preferred_element_type=_F32)
    tot = sum(a.shape[0] for a in lhs_list)
    outs, off = [], 0
    for a in lhs_list:
        m = a.shape[0]
        outs.append(y1[off:off + m] + y1[tot + off:tot + off + m] + y2[off:off + m])
        off += m
    return outs


def _sigmoid(x):
    return 1.0 / (1.0 + jnp.exp(-x))


def _silu(x):
    return x * _sigmoid(x)


def _rms(y, g):
    return y * lax.rsqrt(jnp.mean(y * y, axis=-1, keepdims=True) + RMS_EPS) * g


def _mod_kernel(c_ref, w_ref, b_ref, o_ref):
    c = c_ref[...]
    ch = c.astype(_BF)
    cl = (c - ch.astype(_F32)).astype(_BF)
    y = jnp.dot(jnp.concatenate([ch, cl], axis=0), w_ref[...].astype(_BF),
                preferred_element_type=_F32)
    n = c.shape[0]
    o_ref[...] = y[:n] + y[n:] + b_ref[...]


def _mods(c_all, w_mod, b_mod):
    depth, d, n = w_mod.shape
    rows = c_all.shape[0]
    tn = 1024
    return pl.pallas_call(
        _mod_kernel,
        grid=(depth, n // tn),
        in_specs=[pl.BlockSpec((rows, d), lambda l, j: (0, 0)),
                  pl.BlockSpec((None, d, tn), lambda l, j: (l, 0, j)),
                  pl.BlockSpec((None, 1, tn), lambda l, j: (l, 0, j))],
        out_specs=pl.BlockSpec((None, rows, tn), lambda l, j: (l, 0, j)),
        out_shape=jax.ShapeDtypeStruct((depth, rows, n), _F32),
        compiler_params=_cparams("parallel", "parallel"),
        name="mods",
    )(c_all, w_mod, b_mod.reshape(depth, 1, n))


def _premix_kernel(x_ref, g_ref, sh_ref, sc_ref, w_ref, o_ref, *rest):
    hs_ref = rest[-1]

    @pl.when(pl.program_id(1) == 0)
    def _():
        h = _rms(x_ref[...], g_ref[...]) * (1.0 + sc_ref[...]) + sh_ref[...]
        if len(rest) == 2:
            rest[0][...] = h
        hs_ref[...] = h.astype(_BF)

    o_ref[...] = jnp.dot(hs_ref[...], w_ref[...].astype(_BF), preferred_element_type=_F32)


def _mod_spec(mod, tm, bpb):
    ms = mod.shape[1]
    d = mod.shape[2]
    if ms == 1:
        return pl.BlockSpec((None, 1, d), lambda i, *_: (i // bpb, 0, 0))
    assert mod.shape[0] == 1 and ms == tm
    return pl.BlockSpec((None, ms, d), lambda i, *_: (0, 0, 0))


def _once(block_shape, index_map):
    return pl.BlockSpec(block_shape, index_map, pipeline_mode=pl.Buffered(1))


def _premix(x, g, shift, scale, w, n_out, tm, tn, bpb, emit_h):
    r, d = x.shape
    w, wl = w
    out_specs = [pl.BlockSpec((tm, tn), lambda i, j: (i, j))]
    out_shape = [jax.ShapeDtypeStruct((r, n_out), _F32)]
    if emit_h:
        out_specs.append(pl.BlockSpec((tm, d), lambda i, j: (i, 0)))
        out_shape.append(jax.ShapeDtypeStruct((r, d), _F32))
    return pl.pallas_call(
        _premix_kernel,
        grid=(r // tm, n_out // tn),
        in_specs=[_once((tm, d), lambda i, j: (i, 0)),
                  pl.BlockSpec((1, d), lambda i, j: (0, 0)),
                  _mod_spec(shift, tm, bpb), _mod_spec(scale, tm, bpb),
                  pl.BlockSpec((None, d, tn), lambda i, j: (wl, 0, j))],
        out_specs=out_specs,
        out_shape=out_shape,
        scratch_shapes=[pltpu.VMEM((tm, d), _BF)],
        compiler_params=_cparams("parallel", "arbitrary"),
        name="premix_matmul",
    )(x, g.reshape(1, d), shift, scale, w)


def _plain_mm_kernel(a_ref, w_ref, o_ref):
    o_ref[...] = _dot(a_ref[...], w_ref[...])


def _plain_mm(a, w, n_out, tn):
    m, k = a.shape
    w, wl = w
    return pl.pallas_call(
        _plain_mm_kernel,
        grid=(n_out // tn,),
        in_specs=[pl.BlockSpec((m, k), lambda j: (0, 0)),
                  pl.BlockSpec((None, k, tn), lambda j: (wl, 0, j))],
        out_specs=pl.BlockSpec((m, tn), lambda j: (0, j)),
        out_shape=jax.ShapeDtypeStruct((m, n_out), _F32),
        compiler_params=_cparams("parallel"),
        name="plain_matmul",
    )(a, w)


def _post_kernel(a_ref, w_ref, x_ref, g_ref, gm_ref, o_ref):
    k = pl.program_id(1)

    @pl.when(k == 0)
    def _():
        o_ref[...] = jnp.zeros_like(o_ref)

    o_ref[...] += _dot(a_ref[...], w_ref[...])

    @pl.when(k == pl.num_programs(1) - 1)
    def _():
        o_ref[...] = x_ref[...] + gm_ref[...] * _rms(o_ref[...], g_ref[...])


def _post_mm(a, w, x, g, gate, tm, tk, bpb):
    r, kdim = a.shape
    w, wl = w
    d = w.shape[2]
    return pl.pallas_call(
        _post_kernel,
        grid=(r // tm, kdim // tk),
        in_specs=[pl.BlockSpec((tm, tk), lambda i, k: (i, k)),
                  pl.BlockSpec((None, tk, d), lambda i, k: (wl, k, 0)),
                  _once((tm, d), lambda i, k: (i, 0)),
                  pl.BlockSpec((1, d), lambda i, k: (0, 0)),
                  _mod_spec(gate, tm, bpb)],
        out_specs=pl.BlockSpec((tm, d), lambda i, k: (i, 0)),
        out_shape=jax.ShapeDtypeStruct((r, d), _F32),
        compiler_params=_cparams("parallel", "arbitrary"),
        name="post_matmul",
    )(a, w, x, g.reshape(1, d), gate)


FFN_ROWS = 64


def _ffn_kernel(*refs, seq_mode, bpb, seg):
    if seq_mode:
        (x_ref, gpre_ref, sh_ref, sc_ref, wg_ref, wu_ref, cw_ref, cb_ref, wd_ref, gpost_ref,
         gm_ref, o_ref, gt_ref, hs_ref, carry_ref, gate_ref, up_ref, act_ref) = refs
    else:
        (x_ref, gpre_ref, sh_ref, sc_ref, wg_ref, wu_ref, cw_ref, cb_ref, wd_ref, gpost_ref,
         gm_ref, p1_ref, p2_ref, o_ref, gt_ref, hs_ref) = refs
    i = pl.program_id(0)
    j = pl.program_id(1)

    @pl.when(j == 0)
    def _():
        h = _rms(x_ref[...], gpre_ref[...]) * (1.0 + sc_ref[...]) + sh_ref[...]
        hs_ref[...] = h.astype(_BF)
        o_ref[...] = jnp.zeros_like(o_ref)

    hs = hs_ref[...]
    tm = hs.shape[0]
    cw = cw_ref[...]
    cb = cb_ref[...]
    if seq_mode:
        hdr = SUBLANES

        @pl.when(i % bpb == 0)
        def _():
            carry_ref[j] = jnp.zeros((hdr, gate_ref.shape[1]), _F32)

        gate_ref[:hdr] = carry_ref[j]
        gate_ref[hdr:] = jnp.dot(hs, wg_ref[...].astype(_BF), preferred_element_type=_F32)
        up_ref[...] = jnp.dot(hs, wu_ref[...].astype(_BF), preferred_element_type=_F32)
        tail = gate_ref[tm:]
        carry_ref[j] = tail
        gt_ref[...] = tail
        for c in range(tm // FFN_ROWS):
            r0 = hdr + c * FFN_ROWS
            gc = (cw[0:1] * gate_ref[r0 - 2:r0 - 2 + FFN_ROWS] + cw[1:2] * gate_ref[r0 - 1:r0 - 1 + FFN_ROWS]
                  + cw[2:3] * gate_ref[r0:r0 + FFN_ROWS] + cb)
            act_ref[c * FFN_ROWS:(c + 1) * FFN_ROWS] = (
                _silu(gc) * up_ref[c * FFN_ROWS:(c + 1) * FFN_ROWS]).astype(_BF)
        act = act_ref[...]
    else:
        gate = jnp.dot(hs, wg_ref[...].astype(_BF), preferred_element_type=_F32)
        up = jnp.dot(hs, wu_ref[...].astype(_BF), preferred_element_type=_F32)
        rs = lax.broadcasted_iota(jnp.int32, gate.shape, 0) % seg
        g1 = jnp.where(rs == 0, p1_ref[...], pltpu.roll(gate, 1, axis=0))
        g2 = jnp.where(rs < 2, p2_ref[...], pltpu.roll(gate, 2, axis=0))
        gt_ref[...] = gate
        act = (_silu(cw[0:1] * g2 + cw[1:2] * g1 + cw[2:3] * gate + cb) * up).astype(_BF)
    o_ref[...] += jnp.dot(act, wd_ref[...].astype(_BF), preferred_element_type=_F32)

    @pl.when(j == pl.num_programs(1) - 1)
    def _():
        o_ref[...] = x_ref[...] + gm_ref[...] * _rms(o_ref[...], gpost_ref[...])


def _ffn(x, gpre, shift, scale, wg, wu, cw, cb, wd, gpost, gate, tm, tf, bpb, prev=None, seg=1):
    r, d = x.shape
    (wg, lg), (wu, lu), (wd, ld) = wg, wu, wd
    f = wg.shape[2]
    nblk = r // tm
    seq_mode = prev is None
    gt_rows = SUBLANES if seq_mode else tm
    in_specs = [_once((tm, d), lambda i, j: (i, 0)),
                pl.BlockSpec((1, d), lambda i, j: (0, 0)),
                _mod_spec(shift, tm, bpb), _mod_spec(scale, tm, bpb),
                pl.BlockSpec((None, d, tf), lambda i, j: (lg, 0, j)),
                pl.BlockSpec((None, d, tf), lambda i, j: (lu, 0, j)),
                pl.BlockSpec((cw.shape[0], tf), lambda i, j: (0, j)),
                pl.BlockSpec((1, tf), lambda i, j: (0, j)),
                pl.BlockSpec((None, tf, d), lambda i, j: (ld, j, 0)),
                pl.BlockSpec((1, d), lambda i, j: (0, 0)),
                _mod_spec(gate, tm, bpb)]
    args = [x, gpre.reshape(1, d), shift, scale, wg, wu, cw, cb.reshape(1, f), wd,
            gpost.reshape(1, d), gate]
    scratch = [pltpu.VMEM((tm, d), _BF)]
    if seq_mode:
        assert tm % FFN_ROWS == 0
        scratch += [pltpu.VMEM((f // tf, SUBLANES, tf), _F32),
                    pltpu.VMEM((SUBLANES + tm, tf), _F32), pltpu.VMEM((tm, tf), _F32),
                    pltpu.VMEM((tm, tf), _BF)]
    else:
        in_specs += [pl.BlockSpec((tm, tf), lambda i, j: (i, j))] * 2
        args += list(prev)
    return pl.pallas_call(
        functools.partial(_ffn_kernel, seq_mode=seq_mode, bpb=bpb, seg=seg),
        grid=(nblk, f // tf),
        in_specs=in_specs,
        out_specs=[_once((tm, d), lambda i, j: (i, 0)),
                   pl.BlockSpec((None, gt_rows, tf), lambda i, j: (i, 0, j))],
        out_shape=[jax.ShapeDtypeStruct((r, d), _F32),
                   jax.ShapeDtypeStruct((nblk, gt_rows, f), _F32)],
        scratch_shapes=scratch,
        compiler_params=_cparams("arbitrary", "arbitrary", vmem=VMEM_LIMIT_FFN),
        name="conv_ffn",
    )(*args)


def _prev_rows(cur, first_row_of):
    return first_row_of(pltpu.roll(cur, 1, axis=0))


def _rwkv_prep_kernel(*refs, seq_mode, bpb, seg):
    if seq_mode:
        (h_ref, h8_ref, hst_ref, pa_ref, pa8_ref, past_ref, pb1_ref, pb2_ref) = refs[:8]
        rest = refs[8:]
    else:
        (h_ref, hf_ref, pa_ref, paf_ref, pb1_ref, pb2_ref) = refs[:6]
        rest = refs[6:]
    (murkv_ref, muwag_ref, w0_ref, w1_ref, w2_ref, a0_ref, a1_ref, a2_ref, g1_ref, g2_ref,
     kk_ref, ka_ref, rk_ref, ones_ref,
     r_out, lw_out, k_out, v_out, av_out, bv_out, g_out, bon_out, u_out) = rest
    i = pl.program_id(0)
    h = h_ref[...]
    pa = pa_ref[...]
    rows_h = lax.broadcasted_iota(jnp.int32, h.shape, 0)
    rows_p = lax.broadcasted_iota(jnp.int32, pa.shape, 0)
    hp = pltpu.roll(h, 1, axis=0)
    pp = pltpu.roll(pa, 1, axis=0)
    if seq_mode:
        first = i % bpb == 0
        h0 = jnp.where(first, hst_ref[...], h8_ref[SUBLANES - 1:SUBLANES])
        p0 = jnp.where(first, past_ref[...], pa8_ref[SUBLANES - 1:SUBLANES])
        hp = jnp.where(rows_h == 0, h0, hp)
        pp = jnp.where(rows_p == 0, p0, pp)
    else:
        hp = jnp.where(rows_h % seg == 0, hf_ref[...], hp)
        pp = jnp.where(rows_p % seg == 0, paf_ref[...], pp)
    delta = hp - h
    mu = muwag_ref[...]
    xw = h + delta * mu[0:1]
    xa = h + delta * mu[1:2]
    xg = h + delta * mu[2:3]
    zw = w0_ref[...] + _dot(jnp.tanh(_dot(xw, w1_ref[...])), w2_ref[...])
    w_log = -(jnp.maximum(-zw, 0.0) + jnp.log(1.0 + jnp.exp(-jnp.abs(zw)))) - 0.5
    lw_out[...] = -jnp.exp(w_log)
    a = _sigmoid(a0_ref[...] + _dot(_dot(xa, a1_ref[...]), a2_ref[...]))
    g_out[...] = _dot(_sigmoid(_dot(xg, g1_ref[...])), g2_ref[...])
    rkv = pa + (pp - pa) * murkv_ref[...]
    r = rkv[:, :A_WIDTH]
    k = rkv[:, A_WIDTH:2 * A_WIDTH]
    v = rkv[:, 2 * A_WIDTH:]
    ones = ones_ref[...]

    def segsum(x):
        parts = [_dot_exact_rhs(x[:, c * SCAN_GL:(c + 1) * SCAN_GL], ones)
                 for c in range(A_WIDTH // SCAN_GL)]
        return jnp.concatenate(parts, axis=1)

    kk = k * kk_ref[...]
    kk = kk * lax.rsqrt(jnp.maximum(segsum(kk * kk), 1e-24))
    k2 = k * (1.0 + (a - 1.0) * ka_ref[...])
    r_out[...] = r
    k_out[...] = k2
    v_out[...] = v
    av_out[...] = -kk
    bv_out[...] = kk * a
    bon_out[...] = segsum(r * k2 * rk_ref[...]) * v
    u_out[...] = pb1_ref[...] * _sigmoid(pb2_ref[...])


def _head_ones():
    idx = jnp.arange(SCAN_GL) // A_HEAD
    return (idx[:, None] == idx[None, :]).astype(_BF)


def _rwkv_prep(h, p, W, e, tm, bpb, state=None, first=None, seg=1):
    r, d = h.shape
    a3 = 3 * A_WIDTH
    seq_mode = first is None
    row = lambda n: pl.BlockSpec((tm, n), lambda i: (i, 0))
    full = lambda s: pl.BlockSpec(s, lambda i: (0,) * len(s))
    if seq_mode:
        t8 = tm // SUBLANES
        prev8 = lambda n: pl.BlockSpec((SUBLANES, n), lambda i: (jnp.maximum(i * t8 - 1, 0), 0))
        in_specs = [row(d), prev8(d), pl.BlockSpec((None, 1, d), lambda i: (i // bpb, 0, 0)),
                    row(a3), prev8(a3), pl.BlockSpec((None, 1, a3), lambda i: (i // bpb, 0, 0))]
        args = [h, h, state[0], p, p, state[1]]
    else:
        in_specs = [row(d), row(d), row(a3), row(a3)]
        args = [h, first[0], p, first[1]]
    in_specs += [pl.BlockSpec((tm, B_WIDTH), lambda i: (i, a3 // B_WIDTH)),
                 pl.BlockSpec((tm, B_WIDTH), lambda i: (i, a3 // B_WIDTH + 1))]
    args += [p, p]
    vec = lambda x: x.reshape(1, -1)
    lora = lambda n: -(-n // LANES) * LANES
    pad_c = lambda x: _pad_rows(x, lora(x.shape[1]), axis=1)
    pad_r = lambda x: _pad_rows(x, lora(x.shape[0]), axis=0)
    params = [vec(W['a_mu_rkv'][e]), W['a_mu_wag'][e], vec(W['a_w0'][e]), pad_c(W['a_w1'][e]),
              pad_r(W['a_w2'][e]), vec(W['a_a0'][e]), pad_c(W['a_a1'][e]), pad_r(W['a_a2'][e]),
              W['a_g1'][e],
              W['a_g2'][e], vec(W['a_k_k'][e]), vec(W['a_k_a'][e]), vec(W['a_r_k'][e]),
              _head_ones()]
    in_specs += [full(x.shape) for x in params]
    args += params
    out = jax.ShapeDtypeStruct((r, A_WIDTH), _F32)
    return pl.pallas_call(
        functools.partial(_rwkv_prep_kernel, seq_mode=seq_mode, bpb=bpb, seg=seg),
        grid=(r // tm,),
        in_specs=in_specs,
        out_specs=[row(A_WIDTH)] * 9,
        out_shape=[out] * 9,
        compiler_params=_cparams("parallel"),
        name="rwkv_prep",
    )(*args)


def _scan_groups(rcs, kcs, vcs, lws, avs, bvs, zbds, ltri):
    n = len(rcs)
    ix = range(n)
    c = rcs[0].shape[0]
    gc = SCAN_G * c
    cum = [_dot_exact_rhs_left(ltri, lws[i]) for i in ix]
    cl = [cum[i][c - 1:c] for i in ix]
    e_dn = [jnp.exp(-cum[i]) for i in ix]
    e_cl = [jnp.exp(cl[i] - cum[i]) for i in ix]
    rt = [rcs[i] * jnp.exp(cum[i]) for i in ix]
    at = [avs[i] * jnp.exp(cum[i] - lws[i]) for i in ix]
    kt = [kcs[i] * e_dn[i] for i in ix]
    bt = [bvs[i] * e_dn[i] for i in ix]
    kh = [kcs[i] * e_cl[i] for i in ix]
    bh = [bvs[i] * e_cl[i] for i in ix]

    row_h = lax.broadcasted_iota(jnp.int32, (gc, SCAN_GL), 0) // c
    lane_h = lax.broadcasted_iota(jnp.int32, (gc, SCAN_GL), 1) // A_HEAD
    hm_e = row_h == lane_h

    def expand(x):
        return jnp.where(hm_e, jnp.concatenate([x] * SCAN_G, axis=0), 0.0)

    bd_cc = (lax.broadcasted_iota(jnp.int32, (gc, gc), 0) // c
             == lax.broadcasted_iota(jnp.int32, (gc, gc), 1) // c)

    def bdiag(x):
        return jnp.where(bd_cc, jnp.concatenate([x] * SCAN_G, axis=0), 0.0)

    t_idx = lax.broadcasted_iota(jnp.int32, (c, gc), 0)
    j_idx = lax.broadcasted_iota(jnp.int32, (c, gc), 1) % c
    lo_s = j_idx < t_idx
    lo_i = j_idx <= t_idx
    amat = [_dot_nt(jnp.concatenate([at[i], rt[i]], axis=0),
                    jnp.concatenate([expand(kt[i]), expand(bt[i])], axis=0)) for i in ix]
    a_ak = [jnp.where(lo_s, amat[i][:c, :gc], 0.0) for i in ix]
    a_ab = [jnp.where(lo_s, amat[i][:c, gc:], 0.0) for i in ix]
    a_rk = [jnp.where(lo_i, amat[i][c:, :gc], 0.0) for i in ix]
    a_rb = [jnp.where(lo_i, amat[i][c:, gc:], 0.0) for i in ix]
    eye = jnp.where(j_idx == t_idx, 1.0, 0.0)
    trow = [eye + a_ab[i] for i in ix]
    pw = list(a_ab)
    nlev = c.bit_length() - 1
    for lev in range(nlev):
        for i in ix:
            lhs = ([trow[i]] if lev >= 1 else []) + ([pw[i]] if lev < nlev - 1 else [])
            res = _dot_hilo(lhs, bdiag(pw[i]))
            if lev >= 1:
                trow[i] = trow[i] + res[0]
            if lev < nlev - 1:
                pw[i] = res[-1]
    ve = [expand(vcs[i]) for i in ix]
    x1 = [_dot(a_ak[i], ve[i]) for i in ix]
    tu = [_dot(trow[i], jnp.concatenate([expand(x1[i]), expand(at[i])], axis=1)) for i in ix]
    u0 = [tu[i][:, :SCAN_GL] for i in ix]
    ap = [tu[i][:, SCAN_GL:] for i in ix]
    ru = [_dot(a_rb[i], jnp.concatenate([expand(ap[i]), expand(u0[i])], axis=1)) for i in ix]
    rp = [rt[i] + ru[i][:, :SCAN_GL] for i in ix]
    o0 = [_dot(a_rk[i], ve[i]) + ru[i][:, SCAN_GL:] for i in ix]
    sz = [_dot(jnp.concatenate([rp[i], ap[i]], axis=0), zbds[i]) for i in ix]
    o = [sz[i][:c] + o0[i] for i in ix]
    u = [sz[i][c:] + u0[i] for i in ix]
    dmask = (lax.broadcasted_iota(jnp.int32, (A_HEAD, SCAN_GL), 1) % A_HEAD
             == lax.broadcasted_iota(jnp.int32, (A_HEAD, SCAN_GL), 0))
    bd_ll = (lax.broadcasted_iota(jnp.int32, (SCAN_GL, SCAN_GL), 0) // A_HEAD
             == lax.broadcasted_iota(jnp.int32, (SCAN_GL, SCAN_GL), 1) // A_HEAD)
    znew = []
    for i in ix:
        zc = zbds[i][0:A_HEAD]
        for h in range(1, SCAN_G):
            zc = zc + zbds[i][h * A_HEAD:(h + 1) * A_HEAD]
        dg = jnp.where(dmask, jnp.exp(cl[i]), 0.0)
        xs = jnp.concatenate([kh[i], bh[i], dg], axis=0)
        ys = jnp.concatenate([vcs[i], u[i], zc], axis=0)
        znew.append(jnp.where(bd_ll, _dot_tn(xs, ys), 0.0))
    return o, znew


def _dot_exact_rhs_left(m_bf, x):
    h1, h2, h3 = _split3(x)
    return (jnp.dot(m_bf, h1, preferred_element_type=_F32)
            + jnp.dot(m_bf, h2, preferred_element_type=_F32)
            + jnp.dot(m_bf, h3, preferred_element_type=_F32))


def _scan_kernel(r_ref, k_ref, v_ref, lw_ref, av_ref, bv_ref, z0_ref, o_ref, zout_ref, z_ref):
    t = pl.program_id(1)

    @pl.when(t == 0)
    def _():
        z_ref[...] = z0_ref[...]

    bs, c = r_ref.shape[:2]
    ltri = (lax.broadcasted_iota(jnp.int32, (c, c), 0)
            >= lax.broadcasted_iota(jnp.int32, (c, c), 1)).astype(_BF)
    chains = [(bi, gi) for bi in range(bs) for gi in range(SCAN_NG)]
    lanes = lambda gi: slice(gi * SCAN_GL, (gi + 1) * SCAN_GL)
    tok = lambda ref: [ref[bi, :, lanes(gi)] for bi, gi in chains]
    o, znew = _scan_groups(tok(r_ref), tok(k_ref), tok(v_ref), tok(lw_ref), tok(av_ref),
                           tok(bv_ref), [z_ref[bi, gi] for bi, gi in chains], ltri)
    for n, (bi, gi) in enumerate(chains):
        o_ref[bi, :, lanes(gi)] = o[n]
        z_ref[bi, gi] = znew[n]

    @pl.when(t == pl.num_programs(1) - 1)
    def _():
        zout_ref[...] = z_ref[...]


def _wkv_scan(r, k, v, lw, av, bv, z0, chunk, bs):
    b, t, _ = r.shape
    tok = pl.BlockSpec((bs, chunk, A_WIDTH), lambda bi, ti: (bi, ti, 0))
    zspec = pl.BlockSpec((bs, SCAN_NG, SCAN_GL, SCAN_GL), lambda bi, ti: (bi, 0, 0, 0))
    return pl.pallas_call(
        _scan_kernel,
        grid=(b // bs, t // chunk),
        in_specs=[tok] * 6 + [zspec],
        out_specs=[tok, zspec],
        out_shape=[jax.ShapeDtypeStruct((b, t, A_WIDTH), _F32),
                   jax.ShapeDtypeStruct((b, SCAN_NG, SCAN_GL, SCAN_GL), _F32)],
        scratch_shapes=[pltpu.VMEM((bs, SCAN_NG, SCAN_GL, SCAN_GL), _F32)],
        compiler_params=_cparams("parallel", "arbitrary"),
        name="wkv7_scan",
    )(r, k, v, lw, av, bv, z0)


def _state_to_bd(s):
    b = s.shape[0]
    st = jnp.swapaxes(s, -1, -2).reshape(b, SCAN_NG, SCAN_G, A_HEAD, A_HEAD)
    eye = jnp.eye(SCAN_G, dtype=s.dtype)
    z = st[:, :, :, :, None, :] * eye[None, None, :, None, :, None]
    return z.reshape(b, SCAN_NG, SCAN_GL, SCAN_GL)


def _bd_to_state(z):
    b = z.shape[0]
    z = z.reshape(b, SCAN_NG, SCAN_G, A_HEAD, SCAN_G, A_HEAD)
    idx = jnp.arange(SCAN_G)
    st = z[:, :, idx, :, idx, :]
    st = jnp.moveaxis(st, 0, 2).reshape(b, A_HEADS, A_HEAD, A_HEAD)
    return jnp.swapaxes(st, -1, -2)


def _mix_out_kernel(o_ref, bon_ref, g_ref, lnw_ref, lnb_ref, xm_ref, xh_ref, cw_ref, cb_ref,
                    blw_ref, blb_ref, ones_ref, out_ref, win_ref, sh_ref):
    tm = o_ref.shape[0]
    ones = ones_ref[...]

    def segmean(x):
        parts = [_dot_exact_rhs(x[:, c * SCAN_GL:(c + 1) * SCAN_GL], ones)
                 for c in range(A_WIDTH // SCAN_GL)]
        return jnp.concatenate(parts, axis=1) * (1.0 / A_HEAD)

    o = o_ref[...]
    oc = o - segmean(o)
    var = segmean(oc * oc)
    on = oc * lax.rsqrt(var + A_GN_EPS) * lnw_ref[...] + lnb_ref[...]
    out_ref[:, :A_WIDTH] = (on + bon_ref[...]) * g_ref[...]

    win_ref[:tm] = xm_ref[...]
    win_ref[tm:] = xh_ref[...]
    span = tm + (B_CONV - 1) // SUBLANES * SUBLANES
    for s in range(1, SUBLANES):
        sh_ref[s - 1] = win_ref[s:s + span, :]
    cols = []
    for c in range(B_WIDTH // LANES):
        cs = slice(c * LANES, (c + 1) * LANES)
        acc = jnp.zeros((tm, LANES), _F32) + cb_ref[:, cs]
        for j in range(B_CONV):
            a8, s = j // SUBLANES * SUBLANES, j % SUBLANES
            tap = win_ref[a8:a8 + tm, cs] if s == 0 else sh_ref[s - 1, a8:a8 + tm, cs]
            acc = acc + cw_ref[j:j + 1, cs] * tap
        cols.append(acc)
    ub = jnp.concatenate(cols, axis=1)
    uc = ub - jnp.mean(ub, axis=-1, keepdims=True)
    uv = jnp.mean(uc * uc, axis=-1, keepdims=True)
    out_ref[:, A_WIDTH:] = _silu(uc * lax.rsqrt(uv + LN_EPS) * blw_ref[...] + blb_ref[...])


_CONV_HALO = 32


def _mix_out(o, bon, g, xx, W, e, tm):
    b, t, _ = o.shape
    tok = pl.BlockSpec((None, tm, A_WIDTH), lambda bi, ti: (bi, ti, 0))
    full = lambda s: pl.BlockSpec(s, lambda bi, ti: (0,) * len(s))
    hb = tm // _CONV_HALO
    vec = lambda x: x.reshape(1, -1)
    params = [vec(W['a_ln_w'][e]), vec(W['a_ln_b'][e])]
    conv = [W['b_conv_w'][e], vec(W['b_conv_b'][e]), vec(W['b_ln_w'][e]), vec(W['b_ln_b'][e]),
            _head_ones()]
    return pl.pallas_call(
        _mix_out_kernel,
        grid=(b, t // tm),
        in_specs=[tok, tok, tok] + [full(x.shape) for x in params]
        + [pl.BlockSpec((None, tm, B_WIDTH), lambda bi, ti: (bi, ti, 0)),
           pl.BlockSpec((None, _CONV_HALO, B_WIDTH), lambda bi, ti: (bi, (ti + 1) * hb, 0))]
        + [full(x.shape) for x in conv],
        out_specs=pl.BlockSpec((None, tm, A_WIDTH + B_WIDTH), lambda bi, ti: (bi, ti, 0)),
        out_shape=jax.ShapeDtypeStruct((b, t, A_WIDTH + B_WIDTH), _F32),
        scratch_shapes=[pltpu.VMEM((tm + _CONV_HALO, B_WIDTH), _F32),
                        pltpu.VMEM((SUBLANES - 1, tm + (B_CONV - 1) // SUBLANES * SUBLANES, B_WIDTH),
                                   _F32)],
        compiler_params=_cparams("parallel", "parallel"),
        name="mix_out",
    )(o, bon, g, *params, xx, xx, *conv)


ATT_UNROLL = 4
ATT_TQ = 128
ATT_TB = ATT_TQ * max(C_DILATIONS)


def _attn_seq_kernel(*refs):
    ng = C_GROUPS
    ins = [refs[5 * g:5 * g + 5] for g in range(ng)]
    o_ref = refs[5 * ng]
    kw_refs = refs[5 * ng + 1:5 * ng + 1 + ng]
    vw_refs = refs[5 * ng + 1 + ng:5 * ng + 1 + 2 * ng]
    og_ref, lg_ref = refs[5 * ng + 1 + 2 * ng:]
    ti = pl.program_id(2)
    tq = ATT_TQ
    row = lax.broadcasted_iota(jnp.int32, (tq, 2 * tq), 0)
    col = lax.broadcasted_iota(jnp.int32, (tq, 2 * tq), 1)
    diff = tq + row - col
    band = (diff >= 0) & (diff <= C_NKEYS - 1)
    for g in range(ng):
        d = C_DILATIONS[g]
        q_ref, k_ref, kp_ref, v_ref, vp_ref = ins[g]
        kw, vw = kw_refs[g], vw_refs[g]
        halo = tq * d
        kw[:halo] = kp_ref[...]
        kw[halo:] = k_ref[...]
        vw[:halo] = vp_ref[...]
        vw[halo:] = v_ref[...]
        nsb = ATT_TB // halo

        def tiles(it, carry, d=d, g=g, kw=kw, vw=vw, q_ref=q_ref):
            us = range(ATT_UNROLL)
            ns = [it * ATT_UNROLL + u for u in us]
            sbs = [n // d for n in ns]
            starts = [n % d + (n // d) * (tq * d) for n in ns]
            if d == 1:
                starts = [pl.multiple_of(st, tq) for st in starts]
                rows = lambda st, n_rows: pl.ds(st, n_rows)
            else:
                rows = lambda st, n_rows: pl.ds(st, n_rows, stride=d)
            ss = [_dot_nt(q_ref[rows(starts[u], tq), :], kw[rows(starts[u], 2 * tq), :]) * C_SCALE
                  for u in us]
            ss = [jnp.where(band & ((col >= tq) | (sbs[u] > 0) | (ti > 0)), ss[u], -jnp.inf)
                  for u in us]
            ms = [jnp.max(ss[u], axis=-1, keepdims=True) for u in us]
            ps = [jnp.exp(ss[u] - ms[u]) for u in us]
            ls = [jnp.sum(ps[u], axis=-1, keepdims=True) for u in us]
            os_ = [_dot(ps[u], vw[rows(starts[u], 2 * tq), :]) / ls[u] for u in us]
            for u in us:
                og_ref[g, rows(starts[u], tq), :] = os_[u]
                lg_ref[g, rows(starts[u], tq), :] = jnp.broadcast_to(ms[u] + jnp.log(ls[u]),
                                                                     (tq, C_HEAD))
            return carry

        lax.fori_loop(0, d * nsb // ATT_UNROLL, tiles, 0)
    a0, a1, a2 = lg_ref[0], lg_ref[1], lg_ref[2]
    m = jnp.maximum(jnp.maximum(a0, a1), a2)
    e0, e1, e2 = jnp.exp(a0 - m), jnp.exp(a1 - m), jnp.exp(a2 - m)
    o_ref[...] = (e0 * og_ref[0] + e1 * og_ref[1] + e2 * og_ref[2]) / (e0 + e1 + e2)


def _attn_seq(qkv):
    b, t, n = qkv.shape
    tb = ATT_TB
    in_specs, scratch_k, scratch_v = [], [], []
    for g in range(C_GROUPS):
        halo = ATT_TQ * C_DILATIONS[g]
        per = tb // halo
        colblk = lambda s, g=g: (s * C_GROUPS + g) * C_HEADS
        cur = lambda s, g=g: pl.BlockSpec(
            (None, tb, C_HEAD), lambda bi, hi, ti, c=colblk(s): (bi, ti, c + hi))
        prv = lambda s, g=g, halo=halo, per=per: pl.BlockSpec(
            (None, halo, C_HEAD),
            lambda bi, hi, ti, c=colblk(s): (bi, jnp.maximum(ti * per - 1, 0), c + hi))
        in_specs += [cur(0), cur(1), prv(1), cur(2), prv(2)]
        scratch_k.append(pltpu.VMEM((tb + halo, C_HEAD), _F32))
        scratch_v.append(pltpu.VMEM((tb + halo, C_HEAD), _F32))
    return pl.pallas_call(
        _attn_seq_kernel,
        grid=(b, C_HEADS, t // tb),
        in_specs=in_specs,
        out_specs=pl.BlockSpec((None, tb, C_HEAD), lambda bi, hi, ti: (bi, ti, hi)),
        out_shape=jax.ShapeDtypeStruct((b, t, C_WIDTH), _F32),
        scratch_shapes=scratch_k + scratch_v + [pltpu.VMEM((C_GROUPS, tb, C_HEAD), _F32),
                                               pltpu.VMEM((C_GROUPS, tb, C_HEAD), _F32)],
        compiler_params=_cparams("parallel", "parallel", "parallel"),
        name="attn_seq",
    )(*([qkv] * (5 * C_GROUPS)))


def _attn_dec_kernel(q_ref, kn_ref, vn_ref, c0_ref, c1_ref, c2_ref, o_ref, *, t_new):
    nh = C_HEADS
    rq = q_ref.shape[1]
    outs, lses = [], []
    for gi, c_ref in enumerate((c0_ref, c1_ref, c2_ref)):
        d = C_DILATIONS[gi]
        L = c_ref.shape[1]
        q = q_ref[gi]
        kc = c_ref[0].reshape(L * nh, C_HEAD)
        vc = c_ref[1].reshape(L * nh, C_HEAD)
        s_c = _dot_nt(q, kc) * C_SCALE
        row = lax.broadcasted_iota(jnp.int32, (rq, L * nh), 0)
        colm = lax.broadcasted_iota(jnp.int32, (rq, L * nh), 1)
        dist = L + row // nh - colm // nh
        ok_c = (row % nh == colm % nh) & (dist % d == 0) & (dist <= (C_NKEYS - 1) * d)
        s_c = jnp.where(ok_c, s_c, -jnp.inf)
        m = jnp.max(s_c, axis=-1, keepdims=True)
        qtok = lax.broadcasted_iota(jnp.int32, (rq, 1), 0) // nh
        s_n, v_n = [], []
        for j in range(t_new):
            kj = jnp.concatenate([kn_ref[gi, j * nh:(j + 1) * nh, :]] * t_new, axis=0)
            v_n.append(jnp.concatenate([vn_ref[gi, j * nh:(j + 1) * nh, :]] * t_new, axis=0))
            dn = qtok - j
            ok = (dn >= 0) & (dn % d == 0) & (dn <= (C_NKEYS - 1) * d)
            col = jnp.sum(q * kj, axis=-1, keepdims=True) * C_SCALE
            s_n.append(jnp.where(ok, col, -jnp.inf))
            m = jnp.maximum(m, s_n[j])
        p_c = jnp.exp(s_c - m)
        l = jnp.sum(p_c, axis=-1, keepdims=True)
        o = _dot(p_c, vc)
        for j in range(t_new):
            p_j = jnp.exp(s_n[j] - m)
            l = l + p_j
            o = o + p_j * v_n[j]
        outs.append(o / l)
        lses.append(m + jnp.log(l))
    m = jnp.maximum(jnp.maximum(lses[0], lses[1]), lses[2])
    es = [jnp.exp(x - m) for x in lses]
    o_ref[...] = (es[0] * outs[0] + es[1] * outs[1] + es[2] * outs[2]) / (es[0] + es[1] + es[2])


def _attn_dec(qkv, caches, layer):
    b, t = qkv.shape[:2]
    rq = t * C_HEADS
    x = jnp.transpose(qkv, (2, 0, 3, 1, 4, 5)).reshape(3, b, C_GROUPS, rq, C_HEAD)
    new = lambda s: pl.BlockSpec((None, None, C_GROUPS, rq, C_HEAD), lambda bi: (s, bi, 0, 0, 0))
    cspec = lambda c: _once((None, 2, None) + c.shape[3:], lambda bi: (layer, 0, bi, 0, 0, 0))
    return pl.pallas_call(
        functools.partial(_attn_dec_kernel, t_new=t),
        grid=(b,),
        in_specs=[new(0), new(1), new(2)] + [cspec(c) for c in caches],
        out_specs=pl.BlockSpec((None, rq, C_HEAD), lambda bi: (bi, 0, 0)),
        out_shape=jax.ShapeDtypeStruct((b, rq, C_HEAD), _F32),
        compiler_params=_cparams("parallel"),
        name="attn_decode",
    )(x, x, x, *caches)


def _pad_rows(x, n, axis=1):
    pad = [(0, 0)] * x.ndim
    pad[axis] = (0, n - x.shape[axis])
    return jnp.pad(x, pad)


SEQ_TM = 1024
SEQ_TM_PREP = 256
SEQ_TM_MIX = 128
SEQ_CHUNK = 64
DEC_CHUNK = 32
FFN_TF = 512
QKV_TN = 1024
SEQ_SCAN_BS = 2
DEC_SCAN_BS = 2


def _trunk(x, mods, shift_s, wkv_s, convb_s, ffn_s, kv_s, W, decode):
    bn, t, d = x.shape
    r = bn * t
    xf = x.reshape(r, d)
    if decode:
        tm = r
        bpb = 1
        modv = lambda l, k: jnp.repeat(mods[l, :, k], t, axis=0)[None]
    else:
        tm = min(SEQ_TM, t)
        bpb = t // tm
        modv = lambda l, k: mods[l, :, k][:, None, :]
    new_shift, new_wkv, new_convb, new_ffn = [], [], [], []
    new_kv = [[] for _ in C_WINDOWS]
    a3 = 3 * A_WIDTH
    for l in range(DEPTH):
        if l % 2 == 0:
            e = l // 2
            n_in = W['ab_w_in'].shape[2]
            p, h = _premix(xf, W['g_pre_mix'][l], modv(l, 0), modv(l, 1), (W['ab_w_in'], e),
                           n_in, tm, QKV_TN, bpb, emit_h=True)
            new_shift.append(h.reshape(bn, t, d)[:, -1])
            if decode:
                sp = shift_s[e]
                pa_st = _plain_mm(sp, (W['ab_w_in'], e), a3, 512)
                place = lambda s: jnp.zeros((bn, t, s.shape[-1]), _F32).at[:, 0].set(s).reshape(r, -1)
                outs = _rwkv_prep(h, p, W, e, tm, bpb, first=(place(sp), place(pa_st)), seg=t)
            else:
                tmp = min(SEQ_TM_PREP, t)
                outs = _rwkv_prep(h, p, W, e, tmp, t // tmp,
                                  state=(jnp.zeros((bn, 1, d), _F32), jnp.zeros((bn, 1, a3), _F32)))
            rr, lw, k2, vv, av, bv, gg, bon, u = [o.reshape(bn, t, -1) for o in outs]
            if decode:
                chunk = DEC_CHUNK
                scan_in = [_pad_rows(a_, chunk) for a_ in (rr, k2, vv, lw, av, bv)]
                z0 = _state_to_bd(wkv_s[e].astype(_F32))
            else:
                chunk = SEQ_CHUNK
                scan_in = [rr, k2, vv, lw, av, bv]
                z0 = jnp.zeros((bn, SCAN_NG, SCAN_GL, SCAN_GL), _F32)
            o, zf = _wkv_scan(*scan_in, z0, chunk, DEC_SCAN_BS if decode else SEQ_SCAN_BS)
            new_wkv.append(_bd_to_state(zf))
            cprev = convb_s[e] if decode else jnp.zeros((bn, B_CONV - 1, B_WIDTH), _F32)
            xx = jnp.concatenate([cprev, u], axis=1)
            new_convb.append(xx[:, t:])
            if decode:
                tp = _CONV_HALO
                mo = _mix_out(_pad_rows(o[:, :t], tp), _pad_rows(bon, tp), _pad_rows(gg, tp),
                              _pad_rows(xx, tp + _CONV_HALO), W, e, tp)[:, :t]
            else:
                mo = _mix_out(o, bon, gg, _pad_rows(xx, t + _CONV_HALO), W, e, SEQ_TM_MIX)
            xf = _post_mm(mo.reshape(r, -1), (W['ab_w_out'], e), xf, W['g_post_mix'][l], modv(l, 2),
                          tm, 512, bpb)
        else:
            oi = l // 2
            n_qkv = W['attn_w_qkv'].shape[2]
            qkv, = _premix(xf, W['g_pre_mix'][l], modv(l, 0), modv(l, 1), (W['attn_w_qkv'], oi),
                           n_qkv, tm, QKV_TN, bpb, emit_h=False)
            q5 = qkv.reshape(bn, t, 3, C_GROUPS, C_HEADS, C_HEAD)
            for gi in range(C_GROUPS):
                keep = min(C_WINDOWS[gi], t)
                new_kv[gi].append(jnp.stack([q5[:, t - keep:, 1, gi], q5[:, t - keep:, 2, gi]]))
            if decode:
                att = _attn_dec(q5, kv_s, oi).reshape(r, C_WIDTH)
            else:
                att = _attn_seq(qkv.reshape(bn, t, n_qkv)).reshape(r, C_WIDTH)
            xf = _post_mm(att, (W['attn_w_out'], oi), xf, W['g_post_mix'][l], modv(l, 2), tm, 512, bpb)
        if decode:
            st = ffn_s[l]
            f = st.shape[-1]
            p1 = jnp.zeros((bn, t, f), _F32).at[:, 0].set(st[:, 1]).reshape(r, f)
            p2 = jnp.zeros((bn, t, f), _F32).at[:, 0].set(st[:, 0]).at[:, 1].set(st[:, 1]).reshape(r, f)
            xf, gt = _ffn(xf, W['g_pre_ffn'][l], modv(l, 3), modv(l, 4), (W['ffn_w_gate'], l),
                          (W['ffn_w_up'], l), W['ffn_conv_w'][l], W['ffn_conv_b'][l], (W['ffn_w_down'], l),
                          W['g_post_ffn'][l], modv(l, 5), tm, FFN_TF, bpb, prev=(p1, p2), seg=t)
            new_ffn.append(gt.reshape(bn, t, f)[:, t - 2:])
        else:
            xf, gt = _ffn(xf, W['g_pre_ffn'][l], modv(l, 3), modv(l, 4), (W['ffn_w_gate'], l),
                          (W['ffn_w_up'], l), W['ffn_conv_w'][l], W['ffn_conv_b'][l], (W['ffn_w_down'], l),
                          W['g_post_ffn'][l], modv(l, 5), tm, FFN_TF, bpb)
            f = gt.shape[-1]
            new_ffn.append(gt.reshape(bn, bpb, SUBLANES, f)[:, -1, SUBLANES - 2:])
    return (xf.reshape(bn, t, d), jnp.stack(new_shift), jnp.stack(new_wkv), jnp.stack(new_convb),
            jnp.stack(new_ffn), jnp.stack(new_kv[0]), jnp.stack(new_kv[1]), jnp.stack(new_kv[2]))


def kernel(x_prompt, x_sample, state_shift, state_wkv, state_conv_b, state_ffn, cache_kv_w128, cache_kv_w512, cache_kv_w2048, c_prompt, c_sample, w_mod, b_mod, g_pre_mix, g_post_mix, g_pre_ffn, g_post_ffn, ab_w_in, a_mu_rkv, a_mu_wag, a_w0, a_w1, a_w2, a_a0, a_a1, a_a2, a_g1, a_g2, a_k_k, a_k_a, a_r_k, a_ln_w, a_ln_b, b_conv_w, b_conv_b, b_ln_w, b_ln_b, ab_w_out, attn_w_qkv, attn_w_out, ffn_w_gate, ffn_w_up, ffn_conv_w, ffn_conv_b, ffn_w_down):
    W = dict(w_mod=w_mod, b_mod=b_mod, g_pre_mix=g_pre_mix, g_post_mix=g_post_mix,
             g_pre_ffn=g_pre_ffn, g_post_ffn=g_post_ffn, ab_w_in=ab_w_in, a_mu_rkv=a_mu_rkv,
             a_mu_wag=a_mu_wag, a_w0=a_w0, a_w1=a_w1, a_w2=a_w2, a_a0=a_a0, a_a1=a_a1, a_a2=a_a2,
             a_g1=a_g1, a_g2=a_g2, a_k_k=a_k_k, a_k_a=a_k_a, a_r_k=a_r_k, a_ln_w=a_ln_w, a_ln_b=a_ln_b,
             b_conv_w=b_conv_w, b_conv_b=b_conv_b, b_ln_w=b_ln_w, b_ln_b=b_ln_b, ab_w_out=ab_w_out,
             attn_w_qkv=attn_w_qkv, attn_w_out=attn_w_out, ffn_w_gate=ffn_w_gate, ffn_w_up=ffn_w_up,
             ffn_conv_w=ffn_conv_w, ffn_conv_b=ffn_conv_b, ffn_w_down=ffn_w_down)
    for name in ('ab_w_in', 'ab_w_out', 'attn_w_qkv', 'attn_w_out', 'ffn_w_gate', 'ffn_w_up',
                 'ffn_w_down'):
        W[name] = W[name].astype(_BF)
    nbp = x_prompt.shape[0]
    nbs = x_sample.shape[0]
    d = x_prompt.shape[-1]
    assert d == D_MODEL and w_mod.shape == (DEPTH, d, N_MOD * d)
    c_all = _pad_rows(jnp.concatenate([c_prompt, c_sample], axis=0), 16, axis=0)
    mods = _mods(c_all, w_mod, b_mod).reshape(DEPTH, 16, N_MOD, d)
    outs_p = _trunk(x_prompt, mods[:, :nbp], None, None, None, None, None, W, decode=False)
    outs_s = _trunk(x_sample, mods[:, nbp:nbp + nbs], state_shift, state_wkv, state_conv_b,
                    state_ffn, (cache_kv_w128, cache_kv_w512, cache_kv_w2048), W, decode=True)
    (y_p, p_shift, p_wkv, p_conv_b, p_ffn, p_kv0, p_kv1, p_kv2) = outs_p
    (y_s, s_shift, s_wkv, s_conv_b, s_ffn, s_kv0, s_kv1, s_kv2) = outs_s
    return (y_p, y_s, p_shift, p_wkv, p_conv_b, p_ffn, p_kv0, p_kv1, p_kv2,
            s_shift, s_wkv, s_conv_b, s_ffn, s_kv0, s_kv1, s_kv2)
```

```python
import functools

import jax
import jax.numpy as jnp
from jax import lax
from jax.experimental import pallas as pl
from jax.experimental.pallas import tpu as pltpu

_BF = jnp.bfloat16
_F32 = jnp.float32

D_MODEL = 2048
DEPTH = 4
N_MOD = 6
A_HEAD = 64
A_HEADS = 16
A_WIDTH = A_HEADS * A_HEAD
A_GN_EPS = 64e-5
B_WIDTH = 1024
B_CONV = 31
C_WINDOWS = (128, 512, 2048)
C_DILATIONS = (1, 4, 16)
C_GROUPS = 3
C_HEADS = 8
C_HEAD = 128
C_WIDTH = C_HEADS * C_HEAD
C_SCALE = C_HEAD ** -0.5
C_NKEYS = 129
D_FF = 5632
RMS_EPS = 1e-6
LN_EPS = 1e-5

SUBLANES = 8
LANES = 128
VMEM_LIMIT = 56 * 1024 * 1024
VMEM_LIMIT_FFN = 60 * 1024 * 1024

SCAN_G = 4
SCAN_GL = SCAN_G * A_HEAD
SCAN_NG = A_HEADS // SCAN_G


def _cparams(*sem, vmem=VMEM_LIMIT):
    return pltpu.CompilerParams(dimension_semantics=sem, vmem_limit_bytes=vmem)


def _dot(a, b):
    return jnp.dot(a.astype(_BF), b.astype(_BF), preferred_element_type=_F32)


def _dot_nt(a, b):
    return lax.dot_general(a.astype(_BF), b.astype(_BF), (((1,), (1,)), ((), ())),
                           preferred_element_type=_F32)


def _dot_tn(a, b):
    return lax.dot_general(a.astype(_BF), b.astype(_BF), (((0,), (0,)), ((), ())),
                           preferred_element_type=_F32)


def _split3(x):
    h1 = x.astype(_BF)
    r1 = x - h1.astype(_F32)
    h2 = r1.astype(_BF)
    h3 = (r1 - h2.astype(_F32)).astype(_BF)
    return h1, h2, h3


def _dot_exact_rhs(x, m_bf):
    h1, h2, h3 = _split3(x)
    return (jnp.dot(h1, m_bf, preferred_element_type=_F32)
            + jnp.dot(h2, m_bf, preferred_element_type=_F32)
            + jnp.dot(h3, m_bf, preferred_element_type=_F32))


def _split2(x):
    hi = x.astype(_BF)
    return hi, (x - hi.astype(_F32)).astype(_BF)


def _dot_hilo(lhs_list, rhs):
    his, los = zip(*[_split2(a) for a in lhs_list])
    rh, rl = _split2(rhs)
    y1 = jnp.dot(jnp.concatenate(his + los, axis=0), rh, preferred_element_type=_F32)
    y2 = jnp.dot(jnp.concatenate(his, axis=0), rl, preferred_element_type=_F32)
    tot = sum(a.shape[0] for a in lhs_list)
    outs, off = [], 0
    for a in lhs_list:
        m = a.shape[0]
        outs.append(y1[off:off + m] + y1[tot + off:tot + off + m] + y2[off:off + m])
        off += m
    return outs


def _sigmoid(x):
    return 1.0 / (1.0 + jnp.exp(-x))


def _silu(x):
    return x * _sigmoid(x)


def _rms(y, g):
    return y * lax.rsqrt(jnp.mean(y * y, axis=-1, keepdims=True) + RMS_EPS) * g


def _mod_kernel(c_ref, w_ref, b_ref, o_ref):
    c = c_ref[...]
    ch = c.astype(_BF)
    cl = (c - ch.astype(_F32)).astype(_BF)
    y = jnp.dot(jnp.concatenate([ch, cl], axis=0), w_ref[...].astype(_BF),
                preferred_element_type=_F32)
    n = c.shape[0]
    o_ref[...] = y[:n] + y[n:] + b_ref[...]


def _mods(c_all, w_mod, b_mod):
    depth, d, n = w_mod.shape
    rows = c_all.shape[0]
    tn = 1024
    return pl.pallas_call(
        _mod_kernel,
        grid=(depth, n // tn),
        in_specs=[pl.BlockSpec((rows, d), lambda l, j: (0, 0)),
                  pl.BlockSpec((None, d, tn), lambda l, j: (l, 0, j)),
                  pl.BlockSpec((None, 1, tn), lambda l, j: (l, 0, j))],
        out_specs=pl.BlockSpec((None, rows, tn), lambda l, j: (l, 0, j)),
        out_shape=jax.ShapeDtypeStruct((depth, rows, n), _F32),
        compiler_params=_cparams("parallel", "parallel"),
        name="mods",
    )(c_all, w_mod, b_mod.reshape(depth, 1, n))


NORM_ROWS = 16


def _row_chunks(tm, fn):
    nrow = min(tm, NORM_ROWS)
    for c in range(tm // nrow):
        fn(slice(c * nrow, (c + 1) * nrow))


def _mod_rows(ref, rs):
    return ref[...] if ref.shape[0] == 1 else ref[rs, :]


def _premix_kernel(x_ref, g_ref, sh_ref, sc_ref, w_ref, o_ref, *rest):
    hs_ref = rest[-1]

    @pl.when(pl.program_id(1) == 0)
    def _():
        def chunk(rs):
            h = (_rms(x_ref[rs, :], g_ref[...]) * (1.0 + _mod_rows(sc_ref, rs))
                 + _mod_rows(sh_ref, rs))
            if len(rest) == 2:
                rest[0][rs, :] = h
            hs_ref[rs, :] = h.astype(_BF)

        _row_chunks(hs_ref.shape[0], chunk)

    o_ref[...] = jnp.dot(hs_ref[...], w_ref[...].astype(_BF), preferred_element_type=_F32)


def _mod_spec(mod, tm, bpb):
    ms = mod.shape[1]
    d = mod.shape[2]
    if ms == 1:
        return pl.BlockSpec((None, 1, d), lambda i, *_: (i // bpb, 0, 0))
    assert mod.shape[0] == 1 and ms == tm
    return pl.BlockSpec((None, ms, d), lambda i, *_: (0, 0, 0))


def _once(block_shape, index_map):
    return pl.BlockSpec(block_shape, index_map, pipeline_mode=pl.Buffered(1))


def _premix(x, g, shift, scale, w, n_out, tm, tn, bpb, emit_h):
    r, d = x.shape
    w, wl = w
    out_specs = [pl.BlockSpec((tm, tn), lambda i, j: (i, j))]
    out_shape = [jax.ShapeDtypeStruct((r, n_out), _F32)]
    if emit_h:
        out_specs.append(pl.BlockSpec((tm, d), lambda i, j: (i, 0)))
        out_shape.append(jax.ShapeDtypeStruct((r, d), _F32))
    return pl.pallas_call(
        _premix_kernel,
        grid=(r // tm, n_out // tn),
        in_specs=[_once((tm, d), lambda i, j: (i, 0)),
                  pl.BlockSpec((1, d), lambda i, j: (0, 0)),
                  _mod_spec(shift, tm, bpb), _mod_spec(scale, tm, bpb),
                  pl.BlockSpec((None, d, tn), lambda i, j: (wl, 0, j))],
        out_specs=out_specs,
        out_shape=out_shape,
        scratch_shapes=[pltpu.VMEM((tm, d), _BF)],
        compiler_params=_cparams("parallel", "arbitrary"),
        name="premix_matmul",
    )(x, g.reshape(1, d), shift, scale, w)


def _plain_mm_kernel(a_ref, w_ref, o_ref):
    o_ref[...] = _dot(a_ref[...], w_ref[...])


def _plain_mm(a, w, n_out, tn):
    m, k = a.shape
    w, wl = w
    return pl.pallas_call(
        _plain_mm_kernel,
        grid=(n_out // tn,),
        in_specs=[pl.BlockSpec((m, k), lambda j: (0, 0)),
                  pl.BlockSpec((None, k, tn), lambda j: (wl, 0, j))],
        out_specs=pl.BlockSpec((m, tn), lambda j: (0, j)),
        out_shape=jax.ShapeDtypeStruct((m, n_out), _F32),
        compiler_params=_cparams("parallel"),
        name="plain_matmul",
    )(a, w)


def _post_kernel(a_ref, w_ref, x_ref, g_ref, gm_ref, o_ref):
    k = pl.program_id(1)

    @pl.when(k == 0)
    def _():
        o_ref[...] = jnp.zeros_like(o_ref)

    o_ref[...] += _dot(a_ref[...], w_ref[...])

    @pl.when(k == pl.num_programs(1) - 1)
    def _():
        def chunk(rs):
            o_ref[rs, :] = x_ref[rs, :] + _mod_rows(gm_ref, rs) * _rms(o_ref[rs, :], g_ref[...])

        _row_chunks(o_ref.shape[0], chunk)


def _post_mm(a, w, x, g, gate, tm, tk, bpb):
    r, kdim = a.shape
    w, wl = w
    d = w.shape[2]
    return pl.pallas_call(
        _post_kernel,
        grid=(r // tm, kdim // tk),
        in_specs=[pl.BlockSpec((tm, tk), lambda i, k: (i, k)),
                  pl.BlockSpec((None, tk, d), lambda i, k: (wl, k, 0)),
                  _once((tm, d), lambda i, k: (i, 0)),
                  pl.BlockSpec((1, d), lambda i, k: (0, 0)),
                  _mod_spec(gate, tm, bpb)],
        out_specs=pl.BlockSpec((tm, d), lambda i, k: (i, 0)),
        out_shape=jax.ShapeDtypeStruct((r, d), _F32),
        compiler_params=_cparams("parallel", "arbitrary"),
        name="post_matmul",
    )(a, w, x, g.reshape(1, d), gate)


FFN_ROWS = 64


def _ffn_kernel(*refs, seq_mode, bpb, seg):
    if seq_mode:
        (x_ref, gpre_ref, sh_ref, sc_ref, wg_ref, wu_ref, cw_ref, cb_ref, wd_ref, gpost_ref,
         gm_ref, o_ref, gt_ref, hs_ref, carry_ref, gate_ref, up_ref, act_ref) = refs
    else:
        (x_ref, gpre_ref, sh_ref, sc_ref, wg_ref, wu_ref, cw_ref, cb_ref, wd_ref, gpost_ref,
         gm_ref, p1_ref, p2_ref, o_ref, gt_ref, hs_ref) = refs
    i = pl.program_id(0)
    j = pl.program_id(1)

    tm = hs_ref.shape[0]

    @pl.when(j == 0)
    def _():
        def chunk(rs):
            h = (_rms(x_ref[rs, :], gpre_ref[...]) * (1.0 + _mod_rows(sc_ref, rs))
                 + _mod_rows(sh_ref, rs))
            hs_ref[rs, :] = h.astype(_BF)
            o_ref[rs, :] = jnp.zeros((rs.stop - rs.start, o_ref.shape[1]), _F32)

        _row_chunks(tm, chunk)

    hs = hs_ref[...]
    cw = cw_ref[...]
    cb = cb_ref[...]
    if seq_mode:
        hdr = SUBLANES

        @pl.when(i % bpb == 0)
        def _():
            carry_ref[j] = jnp.zeros((hdr, gate_ref.shape[1]), _F32)

        gate_ref[:hdr] = carry_ref[j]
        gate_ref[hdr:] = jnp.dot(hs, wg_ref[...].astype(_BF), preferred_element_type=_F32)
        up_ref[...] = jnp.dot(hs, wu_ref[...].astype(_BF), preferred_element_type=_F32)
        tail = gate_ref[tm:]
        carry_ref[j] = tail
        gt_ref[...] = tail
        for c in range(tm // FFN_ROWS):
            r0 = hdr + c * FFN_ROWS
            gc = (cw[0:1] * gate_ref[r0 - 2:r0 - 2 + FFN_ROWS] + cw[1:2] * gate_ref[r0 - 1:r0 - 1 + FFN_ROWS]
                  + cw[2:3] * gate_ref[r0:r0 + FFN_ROWS] + cb)
            act_ref[c * FFN_ROWS:(c + 1) * FFN_ROWS] = (
                _silu(gc) * up_ref[c * FFN_ROWS:(c + 1) * FFN_ROWS]).astype(_BF)
        act = act_ref[...]
    else:
        gate = jnp.dot(hs, wg_ref[...].astype(_BF), preferred_element_type=_F32)
        up = jnp.dot(hs, wu_ref[...].astype(_BF), preferred_element_type=_F32)
        rs = lax.broadcasted_iota(jnp.int32, gate.shape, 0) % seg
        g1 = jnp.where(rs == 0, p1_ref[...], pltpu.roll(gate, 1, axis=0))
        g2 = jnp.where(rs < 2, p2_ref[...], pltpu.roll(gate, 2, axis=0))
        gt_ref[...] = gate
        act = (_silu(cw[0:1] * g2 + cw[1:2] * g1 + cw[2:3] * gate + cb) * up).astype(_BF)
    o_ref[...] += jnp.dot(act, wd_ref[...].astype(_BF), preferred_element_type=_F32)

    @pl.when(j == pl.num_programs(1) - 1)
    def _():
        def chunk(rs):
            o_ref[rs, :] = x_ref[rs, :] + _mod_rows(gm_ref, rs) * _rms(o_ref[rs, :], gpost_ref[...])

        _row_chunks(tm, chunk)


def _ffn(x, gpre, shift, scale, wg, wu, cw, cb, wd, gpost, gate, tm, tf, bpb, prev=None, seg=1):
    r, d = x.shape
    (wg, lg), (wu, lu), (wd, ld) = wg, wu, wd
    f = wg.shape[2]
    nblk = r // tm
    seq_mode = prev is None
    gt_rows = SUBLANES if seq_mode else tm
    in_specs = [_once((tm, d), lambda i, j: (i, 0)),
                pl.BlockSpec((1, d), lambda i, j: (0, 0)),
                _mod_spec(shift, tm, bpb), _mod_spec(scale, tm, bpb),
                pl.BlockSpec((None, d, tf), lambda i, j: (lg, 0, j)),
                pl.BlockSpec((None, d, tf), lambda i, j: (lu, 0, j)),
                pl.BlockSpec((cw.shape[0], tf), lambda i, j: (0, j)),
                pl.BlockSpec((1, tf), lambda i, j: (0, j)),
                pl.BlockSpec((None, tf, d), lambda i, j: (ld, j, 0)),
                pl.BlockSpec((1, d), lambda i, j: (0, 0)),
                _mod_spec(gate, tm, bpb)]
    args = [x, gpre.reshape(1, d), shift, scale, wg, wu, cw, cb.reshape(1, f), wd,
            gpost.reshape(1, d), gate]
    scratch = [pltpu.VMEM((tm, d), _BF)]
    if seq_mode:
        assert tm % FFN_ROWS == 0
        scratch += [pltpu.VMEM((f // tf, SUBLANES, tf), _F32),
                    pltpu.VMEM((SUBLANES + tm, tf), _F32), pltpu.VMEM((tm, tf), _F32),
                    pltpu.VMEM((tm, tf), _BF)]
    else:
        in_specs += [pl.BlockSpec((tm, tf), lambda i, j: (i, j))] * 2
        args += list(prev)
    return pl.pallas_call(
        functools.partial(_ffn_kernel, seq_mode=seq_mode, bpb=bpb, seg=seg),
        grid=(nblk, f // tf),
        in_specs=in_specs,
        out_specs=[_once((tm, d), lambda i, j: (i, 0)),
                   pl.BlockSpec((None, gt_rows, tf), lambda i, j: (i, 0, j))],
        out_shape=[jax.ShapeDtypeStruct((r, d), _F32),
                   jax.ShapeDtypeStruct((nblk, gt_rows, f), _F32)],
        scratch_shapes=scratch,
        compiler_params=_cparams("arbitrary", "arbitrary", vmem=VMEM_LIMIT_FFN),
        name="conv_ffn",
    )(*args)


def _prev_rows(cur, first_row_of):
    return first_row_of(pltpu.roll(cur, 1, axis=0))


def _rwkv_prep_kernel(*refs, seq_mode, bpb, seg):
    if seq_mode:
        (h_ref, h8_ref, hst_ref, pa_ref, pa8_ref, past_ref, pb1_ref, pb2_ref) = refs[:8]
        rest = refs[8:]
    else:
        (h_ref, hf_ref, pa_ref, paf_ref, pb1_ref, pb2_ref) = refs[:6]
        rest = refs[6:]
    (murkv_ref, muwag_ref, w0_ref, w1_ref, w2_ref, a0_ref, a1_ref, a2_ref, g1_ref, g2_ref,
     kk_ref, ka_ref, rk_ref, ones_ref,
     r_out, lw_out, k_out, v_out, av_out, bv_out, g_out, bon_out, u_out) = rest
    i = pl.program_id(0)
    h = h_ref[...]
    pa = pa_ref[...]
    rows_h = lax.broadcasted_iota(jnp.int32, h.shape, 0)
    rows_p = lax.broadcasted_iota(jnp.int32, pa.shape, 0)
    hp = pltpu.roll(h, 1, axis=0)
    pp = pltpu.roll(pa, 1, axis=0)
    if seq_mode:
        first = i % bpb == 0
        h0 = jnp.where(first, hst_ref[...], h8_ref[SUBLANES - 1:SUBLANES])
        p0 = jnp.where(first, past_ref[...], pa8_ref[SUBLANES - 1:SUBLANES])
        hp = jnp.where(rows_h == 0, h0, hp)
        pp = jnp.where(rows_p == 0, p0, pp)
    else:
        hp = jnp.where(rows_h % seg == 0, hf_ref[...], hp)
        pp = jnp.where(rows_p % seg == 0, paf_ref[...], pp)
    delta = hp - h
    mu = muwag_ref[...]
    xw = h + delta * mu[0:1]
    xa = h + delta * mu[1:2]
    xg = h + delta * mu[2:3]
    zw = w0_ref[...] + _dot(jnp.tanh(_dot(xw, w1_ref[...])), w2_ref[...])
    w_log = -(jnp.maximum(-zw, 0.0) + jnp.log(1.0 + jnp.exp(-jnp.abs(zw)))) - 0.5
    lw_out[...] = -jnp.exp(w_log)
    a = _sigmoid(a0_ref[...] + _dot(_dot(xa, a1_ref[...]), a2_ref[...]))
    g_out[...] = _dot(_sigmoid(_dot(xg, g1_ref[...])), g2_ref[...])
    rkv = pa + (pp - pa) * murkv_ref[...]
    r = rkv[:, :A_WIDTH]
    k = rkv[:, A_WIDTH:2 * A_WIDTH]
    v = rkv[:, 2 * A_WIDTH:]
    ones = ones_ref[...]

    def segsum(x):
        parts = [_dot_exact_rhs(x[:, c * SCAN_GL:(c + 1) * SCAN_GL], ones)
                 for c in range(A_WIDTH // SCAN_GL)]
        return jnp.concatenate(parts, axis=1)

    kk = k * kk_ref[...]
    kk = kk * lax.rsqrt(jnp.maximum(segsum(kk * kk), 1e-24))
    k2 = k * (1.0 + (a - 1.0) * ka_ref[...])
    r_out[...] = r
    k_out[...] = k2
    v_out[...] = v
    av_out[...] = -kk
    bv_out[...] = kk * a
    bon_out[...] = segsum(r * k2 * rk_ref[...]) * v
    u_out[...] = pb1_ref[...] * _sigmoid(pb2_ref[...])


def _head_ones():
    idx = jnp.arange(SCAN_GL) // A_HEAD
    return (idx[:, None] == idx[None, :]).astype(_BF)


def _rwkv_prep(h, p, W, e, tm, bpb, state=None, first=None, seg=1):
    r, d = h.shape
    a3 = 3 * A_WIDTH
    seq_mode = first is None
    row = lambda n: pl.BlockSpec((tm, n), lambda i: (i, 0))
    full = lambda s: pl.BlockSpec(s, lambda i: (0,) * len(s))
    if seq_mode:
        t8 = tm // SUBLANES
        prev8 = lambda n: pl.BlockSpec((SUBLANES, n), lambda i: (jnp.maximum(i * t8 - 1, 0), 0))
        in_specs = [row(d), prev8(d), pl.BlockSpec((None, 1, d), lambda i: (i // bpb, 0, 0)),
                    row(a3), prev8(a3), pl.BlockSpec((None, 1, a3), lambda i: (i // bpb, 0, 0))]
        args = [h, h, state[0], p, p, state[1]]
    else:
        in_specs = [row(d), row(d), row(a3), row(a3)]
        args = [h, first[0], p, first[1]]
    in_specs += [pl.BlockSpec((tm, B_WIDTH), lambda i: (i, a3 // B_WIDTH)),
                 pl.BlockSpec((tm, B_WIDTH), lambda i: (i, a3 // B_WIDTH + 1))]
    args += [p, p]
    vec = lambda x: x.reshape(1, -1)
    lora = lambda n: -(-n // LANES) * LANES
    pad_c = lambda x: _pad_rows(x, lora(x.shape[1]), axis=1)
    pad_r = lambda x: _pad_rows(x, lora(x.shape[0]), axis=0)
    params = [vec(W['a_mu_rkv'][e]), W['a_mu_wag'][e], vec(W['a_w0'][e]), pad_c(W['a_w1'][e]),
              pad_r(W['a_w2'][e]), vec(W['a_a0'][e]), pad_c(W['a_a1'][e]), pad_r(W['a_a2'][e]),
              W['a_g1'][e],
              W['a_g2'][e], vec(W['a_k_k'][e]), vec(W['a_k_a'][e]), vec(W['a_r_k'][e]),
              _head_ones()]
    in_specs += [full(x.shape) for x in params]
    args += params
    out = jax.ShapeDtypeStruct((r, A_WIDTH), _F32)
    return pl.pallas_call(
        functools.partial(_rwkv_prep_kernel, seq_mode=seq_mode, bpb=bpb, seg=seg),
        grid=(r // tm,),
        in_specs=in_specs,
        out_specs=[row(A_WIDTH)] * 9,
        out_shape=[out] * 9,
        compiler_params=_cparams("parallel"),
        name="rwkv_prep",
    )(*args)


def _scan_groups(rcs, kcs, vcs, lws, avs, bvs, zbds, ltri):
    n = len(rcs)
    ix = range(n)
    c = rcs[0].shape[0]
    gc = SCAN_G * c
    cum = [_dot_exact_rhs_left(ltri, lws[i]) for i in ix]
    cl = [cum[i][c - 1:c] for i in ix]
    e_dn = [jnp.exp(-cum[i]) for i in ix]
    e_cl = [jnp.exp(cl[i] - cum[i]) for i in ix]
    rt = [rcs[i] * jnp.exp(cum[i]) for i in ix]
    at = [avs[i] * jnp.exp(cum[i] - lws[i]) for i in ix]
    kt = [kcs[i] * e_dn[i] for i in ix]
    bt = [bvs[i] * e_dn[i] for i in ix]
    kh = [kcs[i] * e_cl[i] for i in ix]
    bh = [bvs[i] * e_cl[i] for i in ix]

    row_h = lax.broadcasted_iota(jnp.int32, (gc, SCAN_GL), 0) // c
    lane_h = lax.broadcasted_iota(jnp.int32, (gc, SCAN_GL), 1) // A_HEAD
    hm_e = row_h == lane_h

    def expand(x):
        return jnp.where(hm_e, jnp.concatenate([x] * SCAN_G, axis=0), 0.0)

    bd_cc = (lax.broadcasted_iota(jnp.int32, (gc, gc), 0) // c
             == lax.broadcasted_iota(jnp.int32, (gc, gc), 1) // c)

    def bdiag(x):
        return jnp.where(bd_cc, jnp.concatenate([x] * SCAN_G, axis=0), 0.0)

    t_idx = lax.broadcasted_iota(jnp.int32, (c, gc), 0)
    j_idx = lax.broadcasted_iota(jnp.int32, (c, gc), 1) % c
    lo_s = j_idx < t_idx
    lo_i = j_idx <= t_idx
    amat = [_dot_nt(jnp.concatenate([at[i], rt[i]], axis=0),
                    jnp.concatenate([expand(kt[i]), expand(bt[i])], axis=0)) for i in ix]
    a_ak = [jnp.where(lo_s, amat[i][:c, :gc], 0.0) for i in ix]
    a_ab = [jnp.where(lo_s, amat[i][:c, gc:], 0.0) for i in ix]
    a_rk = [jnp.where(lo_i, amat[i][c:, :gc], 0.0) for i in ix]
    a_rb = [jnp.where(lo_i, amat[i][c:, gc:], 0.0) for i in ix]
    eye = jnp.where(j_idx == t_idx, 1.0, 0.0)
    trow = [eye + a_ab[i] for i in ix]
    pw = list(a_ab)
    nlev = c.bit_length() - 1
    for lev in range(nlev):
        for i in ix:
            lhs = ([trow[i]] if lev >= 1 else []) + ([pw[i]] if lev < nlev - 1 else [])
            res = _dot_hilo(lhs, bdiag(pw[i]))
            if lev >= 1:
                trow[i] = trow[i] + res[0]
            if lev < nlev - 1:
                pw[i] = res[-1]
    ve = [expand(vcs[i]) for i in ix]
    x1 = [_dot(a_ak[i], ve[i]) for i in ix]
    tu = [_dot(trow[i], jnp.concatenate([expand(x1[i]), expand(at[i])], axis=1)) for i in ix]
    u0 = [tu[i][:, :SCAN_GL] for i in ix]
    ap = [tu[i][:, SCAN_GL:] for i in ix]
    ru = [_dot(a_rb[i], jnp.concatenate([expand(ap[i]), expand(u0[i])], axis=1)) for i in ix]
    rp = [rt[i] + ru[i][:, :SCAN_GL] for i in ix]
    o0 = [_dot(a_rk[i], ve[i]) + ru[i][:, SCAN_GL:] for i in ix]
    sz = [_dot(jnp.concatenate([rp[i], ap[i]], axis=0), zbds[i]) for i in ix]
    o = [sz[i][:c] + o0[i] for i in ix]
    u = [sz[i][c:] + u0[i] for i in ix]
    dmask = (lax.broadcasted_iota(jnp.int32, (A_HEAD, SCAN_GL), 1) % A_HEAD
             == lax.broadcasted_iota(jnp.int32, (A_HEAD, SCAN_GL), 0))
    bd_ll = (lax.broadcasted_iota(jnp.int32, (SCAN_GL, SCAN_GL), 0) // A_HEAD
             == lax.broadcasted_iota(jnp.int32, (SCAN_GL, SCAN_GL), 1) // A_HEAD)
    znew = []
    for i in ix:
        zc = zbds[i][0:A_HEAD]
        for h in range(1, SCAN_G):
            zc = zc + zbds[i][h * A_HEAD:(h + 1) * A_HEAD]
        dg = jnp.where(dmask, jnp.exp(cl[i]), 0.0)
        xs = jnp.concatenate([kh[i], bh[i], dg], axis=0)
        ys = jnp.concatenate([vcs[i], u[i], zc], axis=0)
        znew.append(jnp.where(bd_ll, _dot_tn(xs, ys), 0.0))
    return o, znew


def _dot_exact_rhs_left(m_bf, x):
    h1, h2, h3 = _split3(x)
    return (jnp.dot(m_bf, h1, preferred_element_type=_F32)
            + jnp.dot(m_bf, h2, preferred_element_type=_F32)
            + jnp.dot(m_bf, h3, preferred_element_type=_F32))


def _scan_kernel(r_ref, k_ref, v_ref, lw_ref, av_ref, bv_ref, z0_ref, o_ref, zout_ref, z_ref):
    t = pl.program_id(1)

    @pl.when(t == 0)
    def _():
        z_ref[...] = z0_ref[...]

    bs, c = r_ref.shape[:2]
    ltri = (lax.broadcasted_iota(jnp.int32, (c, c), 0)
            >= lax.broadcasted_iota(jnp.int32, (c, c), 1)).astype(_BF)
    chains = [(bi, gi) for bi in range(bs) for gi in range(SCAN_NG)]
    lanes = lambda gi: slice(gi * SCAN_GL, (gi + 1) * SCAN_GL)
    tok = lambda ref: [ref[bi, :, lanes(gi)] for bi, gi in chains]
    o, znew = _scan_groups(tok(r_ref), tok(k_ref), tok(v_ref), tok(lw_ref), tok(av_ref),
                           tok(bv_ref), [z_ref[bi, gi] for bi, gi in chains], ltri)
    for n, (bi, gi) in enumerate(chains):
        o_ref[bi, :, lanes(gi)] = o[n]
        z_ref[bi, gi] = znew[n]

    @pl.when(t == pl.num_programs(1) - 1)
    def _():
        zout_ref[...] = z_ref[...]


def _wkv_scan(r, k, v, lw, av, bv, z0, chunk, bs):
    b, t, _ = r.shape
    tok = pl.BlockSpec((bs, chunk, A_WIDTH), lambda bi, ti: (bi, ti, 0))
    zspec = pl.BlockSpec((bs, SCAN_NG, SCAN_GL, SCAN_GL), lambda bi, ti: (bi, 0, 0, 0))
    return pl.pallas_call(
        _scan_kernel,
        grid=(b // bs, t // chunk),
        in_specs=[tok] * 6 + [zspec],
        out_specs=[tok, zspec],
        out_shape=[jax.ShapeDtypeStruct((b, t, A_WIDTH), _F32),
                   jax.ShapeDtypeStruct((b, SCAN_NG, SCAN_GL, SCAN_GL), _F32)],
        scratch_shapes=[pltpu.VMEM((bs, SCAN_NG, SCAN_GL, SCAN_GL), _F32)],
        compiler_params=_cparams("parallel", "arbitrary"),
        name="wkv7_scan",
    )(r, k, v, lw, av, bv, z0)


def _state_to_bd(s):
    b = s.shape[0]
    st = jnp.swapaxes(s, -1, -2).reshape(b, SCAN_NG, SCAN_G, A_HEAD, A_HEAD)
    eye = jnp.eye(SCAN_G, dtype=s.dtype)
    z = st[:, :, :, :, None, :] * eye[None, None, :, None, :, None]
    return z.reshape(b, SCAN_NG, SCAN_GL, SCAN_GL)


def _bd_to_state(z):
    b = z.shape[0]
    z = z.reshape(b, SCAN_NG, SCAN_G, A_HEAD, SCAN_G, A_HEAD)
    idx = jnp.arange(SCAN_G)
    st = z[:, :, idx, :, idx, :]
    st = jnp.moveaxis(st, 0, 2).reshape(b, A_HEADS, A_HEAD, A_HEAD)
    return jnp.swapaxes(st, -1, -2)


def _mix_out_kernel(o_ref, bon_ref, g_ref, lnw_ref, lnb_ref, xm_ref, xh_ref, cw_ref, cb_ref,
                    blw_ref, blb_ref, ones_ref, out_ref, win_ref, sh_ref):
    tm = o_ref.shape[0]
    ones = ones_ref[...]

    def segmean(x):
        parts = [_dot_exact_rhs(x[:, c * SCAN_GL:(c + 1) * SCAN_GL], ones)
                 for c in range(A_WIDTH // SCAN_GL)]
        return jnp.concatenate(parts, axis=1) * (1.0 / A_HEAD)

    o = o_ref[...]
    oc = o - segmean(o)
    var = segmean(oc * oc)
    on = oc * lax.rsqrt(var + A_GN_EPS) * lnw_ref[...] + lnb_ref[...]
    out_ref[:, :A_WIDTH] = (on + bon_ref[...]) * g_ref[...]

    win_ref[:tm] = xm_ref[...]
    win_ref[tm:] = xh_ref[...]
    span = tm + (B_CONV - 1) // SUBLANES * SUBLANES
    for s in range(1, SUBLANES):
        sh_ref[s - 1] = win_ref[s:s + span, :]
    cols = []
    for c in range(B_WIDTH // LANES):
        cs = slice(c * LANES, (c + 1) * LANES)
        acc = jnp.zeros((tm, LANES), _F32) + cb_ref[:, cs]
        for j in range(B_CONV):
            a8, s = j // SUBLANES * SUBLANES, j % SUBLANES
            tap = win_ref[a8:a8 + tm, cs] if s == 0 else sh_ref[s - 1, a8:a8 + tm, cs]
            acc = acc + cw_ref[j:j + 1, cs] * tap
        cols.append(acc)
    ub = jnp.concatenate(cols, axis=1)
    uc = ub - jnp.mean(ub, axis=-1, keepdims=True)
    uv = jnp.mean(uc * uc, axis=-1, keepdims=True)
    out_ref[:, A_WIDTH:] = _silu(uc * lax.rsqrt(uv + LN_EPS) * blw_ref[...] + blb_ref[...])


_CONV_HALO = 32


def _mix_out(o, bon, g, xx, W, e, tm):
    b, t, _ = o.shape
    tok = pl.BlockSpec((None, tm, A_WIDTH), lambda bi, ti: (bi, ti, 0))
    full = lambda s: pl.BlockSpec(s, lambda bi, ti: (0,) * len(s))
    hb = tm // _CONV_HALO
    vec = lambda x: x.reshape(1, -1)
    params = [vec(W['a_ln_w'][e]), vec(W['a_ln_b'][e])]
    conv = [W['b_conv_w'][e], vec(W['b_conv_b'][e]), vec(W['b_ln_w'][e]), vec(W['b_ln_b'][e]),
            _head_ones()]
    return pl.pallas_call(
        _mix_out_kernel,
        grid=(b, t // tm),
        in_specs=[tok, tok, tok] + [full(x.shape) for x in params]
        + [pl.BlockSpec((None, tm, B_WIDTH), lambda bi, ti: (bi, ti, 0)),
           pl.BlockSpec((None, _CONV_HALO, B_WIDTH), lambda bi, ti: (bi, (ti + 1) * hb, 0))]
        + [full(x.shape) for x in conv],
        out_specs=pl.BlockSpec((None, tm, A_WIDTH + B_WIDTH), lambda bi, ti: (bi, ti, 0)),
        out_shape=jax.ShapeDtypeStruct((b, t, A_WIDTH + B_WIDTH), _F32),
        scratch_shapes=[pltpu.VMEM((tm + _CONV_HALO, B_WIDTH), _F32),
                        pltpu.VMEM((SUBLANES - 1, tm + (B_CONV - 1) // SUBLANES * SUBLANES, B_WIDTH),
                                   _F32)],
        compiler_params=_cparams("parallel", "parallel"),
        name="mix_out",
    )(o, bon, g, *params, xx, xx, *conv)


ATT_UNROLL = 4
ATT_TQ = 128
ATT_TB = ATT_TQ * max(C_DILATIONS)


def _attn_seq_kernel(*refs):
    ng = C_GROUPS
    ins = [refs[5 * g:5 * g + 5] for g in range(ng)]
    o_ref = refs[5 * ng]
    kw_refs = refs[5 * ng + 1:5 * ng + 1 + ng]
    vw_refs = refs[5 * ng + 1 + ng:5 * ng + 1 + 2 * ng]
    og_ref, lg_ref = refs[5 * ng + 1 + 2 * ng:]
    ti = pl.program_id(2)
    tq = ATT_TQ
    row = lax.broadcasted_iota(jnp.int32, (tq, 2 * tq), 0)
    col = lax.broadcasted_iota(jnp.int32, (tq, 2 * tq), 1)
    diff = tq + row - col
    band = (diff >= 0) & (diff <= C_NKEYS - 1)
    for g in range(ng):
        d = C_DILATIONS[g]
        q_ref, k_ref, kp_ref, v_ref, vp_ref = ins[g]
        kw, vw = kw_refs[g], vw_refs[g]
        halo = tq * d
        kw[:halo] = kp_ref[...]
        kw[halo:] = k_ref[...]
        vw[:halo] = vp_ref[...]
        vw[halo:] = v_ref[...]
        nsb = ATT_TB // halo

        def tiles(it, carry, d=d, g=g, kw=kw, vw=vw, q_ref=q_ref):
            us = range(ATT_UNROLL)
            ns = [it * ATT_UNROLL + u for u in us]
            sbs = [n // d for n in ns]
            starts = [n % d + (n // d) * (tq * d) for n in ns]
            if d == 1:
                starts = [pl.multiple_of(st, tq) for st in starts]
                rows = lambda st, n_rows: pl.ds(st, n_rows)
            else:
                rows = lambda st, n_rows: pl.ds(st, n_rows, stride=d)
            ss = [_dot_nt(q_ref[rows(starts[u], tq), :], kw[rows(starts[u], 2 * tq), :]) * C_SCALE
                  for u in us]
            ss = [jnp.where(band & ((col >= tq) | (sbs[u] > 0) | (ti > 0)), ss[u], -jnp.inf)
                  for u in us]
            ms = [jnp.max(ss[u], axis=-1, keepdims=True) for u in us]
            ps = [jnp.exp(ss[u] - ms[u]) for u in us]
            ls = [jnp.sum(ps[u], axis=-1, keepdims=True) for u in us]
            os_ = [_dot(ps[u], vw[rows(starts[u], 2 * tq), :]) / ls[u] for u in us]
            for u in us:
                og_ref[g, rows(starts[u], tq), :] = os_[u]
                lg_ref[g, rows(starts[u], tq), :] = jnp.broadcast_to(ms[u] + jnp.log(ls[u]),
                                                                     (tq, C_HEAD))
            return carry

        lax.fori_loop(0, d * nsb // ATT_UNROLL, tiles, 0)
    a0, a1, a2 = lg_ref[0], lg_ref[1], lg_ref[2]
    m = jnp.maximum(jnp.maximum(a0, a1), a2)
    e0, e1, e2 = jnp.exp(a0 - m), jnp.exp(a1 - m), jnp.exp(a2 - m)
    o_ref[...] = (e0 * og_ref[0] + e1 * og_ref[1] + e2 * og_ref[2]) / (e0 + e1 + e2)


def _attn_seq(qkv):
    b, t, n = qkv.shape
    tb = ATT_TB
    in_specs, scratch_k, scratch_v = [], [], []
    for g in range(C_GROUPS):
        halo = ATT_TQ * C_DILATIONS[g]
        per = tb // halo
        colblk = lambda s, g=g: (s * C_GROUPS + g) * C_HEADS
        cur = lambda s, g=g: pl.BlockSpec(
            (None, tb, C_HEAD), lambda bi, hi, ti, c=colblk(s): (bi, ti, c + hi))
        prv = lambda s, g=g, halo=halo, per=per: pl.BlockSpec(
            (None, halo, C_HEAD),
            lambda bi, hi, ti, c=colblk(s): (bi, jnp.maximum(ti * per - 1, 0), c + hi))
        in_specs += [cur(0), cur(1), prv(1), cur(2), prv(2)]
        scratch_k.append(pltpu.VMEM((tb + halo, C_HEAD), _F32))
        scratch_v.append(pltpu.VMEM((tb + halo, C_HEAD), _F32))
    return pl.pallas_call(
        _attn_seq_kernel,
        grid=(b, C_HEADS, t // tb),
        in_specs=in_specs,
        out_specs=pl.BlockSpec((None, tb, C_HEAD), lambda bi, hi, ti: (bi, ti, hi)),
        out_shape=jax.ShapeDtypeStruct((b, t, C_WIDTH), _F32),
        scratch_shapes=scratch_k + scratch_v + [pltpu.VMEM((C_GROUPS, tb, C_HEAD), _F32),
                                               pltpu.VMEM((C_GROUPS, tb, C_HEAD), _F32)],
        compiler_params=_cparams("parallel", "parallel", "parallel"),
        name="attn_seq",
    )(*([qkv] * (5 * C_GROUPS)))


def _attn_dec_kernel(q_ref, kn_ref, vn_ref, c0_ref, c1_ref, c2_ref, o_ref, *, t_new):
    nh = C_HEADS
    rq = q_ref.shape[1]
    outs, lses = [], []
    for gi, c_ref in enumerate((c0_ref, c1_ref, c2_ref)):
        d = C_DILATIONS[gi]
        L = c_ref.shape[1]
        q = q_ref[gi]
        kc = c_ref[0].reshape(L * nh, C_HEAD)
        vc = c_ref[1].reshape(L * nh, C_HEAD)
        s_c = _dot_nt(q, kc) * C_SCALE
        row = lax.broadcasted_iota(jnp.int32, (rq, L * nh), 0)
        colm = lax.broadcasted_iota(jnp.int32, (rq, L * nh), 1)
        dist = L + row // nh - colm // nh
        ok_c = (row % nh == colm % nh) & (dist % d == 0) & (dist <= (C_NKEYS - 1) * d)
        s_c = jnp.where(ok_c, s_c, -jnp.inf)
        m = jnp.max(s_c, axis=-1, keepdims=True)
        qtok = lax.broadcasted_iota(jnp.int32, (rq, 1), 0) // nh
        s_n, v_n = [], []
        for j in range(t_new):
            kj = jnp.concatenate([kn_ref[gi, j * nh:(j + 1) * nh, :]] * t_new, axis=0)
            v_n.append(jnp.concatenate([vn_ref[gi, j * nh:(j + 1) * nh, :]] * t_new, axis=0))
            dn = qtok - j
            ok = (dn >= 0) & (dn % d == 0) & (dn <= (C_NKEYS - 1) * d)
            col = jnp.sum(q * kj, axis=-1, keepdims=True) * C_SCALE
            s_n.append(jnp.where(ok, col, -jnp.inf))
            m = jnp.maximum(m, s_n[j])
        p_c = jnp.exp(s_c - m)
        l = jnp.sum(p_c, axis=-1, keepdims=True)
        o = _dot(p_c, vc)
        for j in range(t_new):
            p_j = jnp.exp(s_n[j] - m)
            l = l + p_j
            o = o + p_j * v_n[j]
        outs.append(o / l)
        lses.append(m + jnp.log(l))
    m = jnp.maximum(jnp.maximum(lses[0], lses[1]), lses[2])
    es = [jnp.exp(x - m) for x in lses]
    o_ref[...] = (es[0] * outs[0] + es[1] * outs[1] + es[2] * outs[2]) / (es[0] + es[1] + es[2])


def _attn_dec(qkv, caches, layer):
    b, t = qkv.shape[:2]
    rq = t * C_HEADS
    x = jnp.transpose(qkv, (2, 0, 3, 1, 4, 5)).reshape(3, b, C_GROUPS, rq, C_HEAD)
    new = lambda s: pl.BlockSpec((None, None, C_GROUPS, rq, C_HEAD), lambda bi: (s, bi, 0, 0, 0))
    cspec = lambda c: _once((None, 2, None) + c.shape[3:], lambda bi: (layer, 0, bi, 0, 0, 0))
    return pl.pallas_call(
        functools.partial(_attn_dec_kernel, t_new=t),
        grid=(b,),
        in_specs=[new(0), new(1), new(2)] + [cspec(c) for c in caches],
        out_specs=pl.BlockSpec((None, rq, C_HEAD), lambda bi: (bi, 0, 0)),
        out_shape=jax.ShapeDtypeStruct((b, rq, C_HEAD), _F32),
        compiler_params=_cparams("parallel"),
        name="attn_decode",
    )(x, x, x, *caches)


def _pad_rows(x, n, axis=1):
    pad = [(0, 0)] * x.ndim
    pad[axis] = (0, n - x.shape[axis])
    return jnp.pad(x, pad)


SEQ_TM = 1024
SEQ_TM_PREP = 256
SEQ_TM_MIX = 128
SEQ_CHUNK = 64
DEC_CHUNK = 32
FFN_TF = 512
QKV_TN = 1024
POST_TK = 1024
SEQ_SCAN_BS = 2
DEC_SCAN_BS = 2


def _trunk(x, mods, shift_s, wkv_s, convb_s, ffn_s, kv_s, W, decode):
    bn, t, d = x.shape
    r = bn * t
    xf = x.reshape(r, d)
    if decode:
        tm = r
        bpb = 1
        modv = lambda l, k: jnp.repeat(mods[l, :, k], t, axis=0)[None]
    else:
        tm = min(SEQ_TM, t)
        bpb = t // tm
        modv = lambda l, k: mods[l, :, k][:, None, :]
    new_shift, new_wkv, new_convb, new_ffn = [], [], [], []
    new_kv = [[] for _ in C_WINDOWS]
    a3 = 3 * A_WIDTH
    for l in range(DEPTH):
        if l % 2 == 0:
            e = l // 2
            n_in = W['ab_w_in'].shape[2]
            p, h = _premix(xf, W['g_pre_mix'][l], modv(l, 0), modv(l, 1), (W['ab_w_in'], e),
                           n_in, tm, 512, bpb, emit_h=True)
            new_shift.append(h.reshape(bn, t, d)[:, -1])
            if decode:
                sp = shift_s[e]
                pa_st = _plain_mm(sp, (W['ab_w_in'], e), a3, 512)
                place = lambda s: jnp.zeros((bn, t, s.shape[-1]), _F32).at[:, 0].set(s).reshape(r, -1)
                outs = _rwkv_prep(h, p, W, e, tm, bpb, first=(place(sp), place(pa_st)), seg=t)
            else:
                tmp = min(SEQ_TM_PREP, t)
                outs = _rwkv_prep(h, p, W, e, tmp, t // tmp,
                                  state=(jnp.zeros((bn, 1, d), _F32), jnp.zeros((bn, 1, a3), _F32)))
            rr, lw, k2, vv, av, bv, gg, bon, u = [o.reshape(bn, t, -1) for o in outs]
            if decode:
                chunk = DEC_CHUNK
                scan_in = [_pad_rows(a_, chunk) for a_ in (rr, k2, vv, lw, av, bv)]
                z0 = _state_to_bd(wkv_s[e].astype(_F32))
            else:
                chunk = SEQ_CHUNK
                scan_in = [rr, k2, vv, lw, av, bv]
                z0 = jnp.zeros((bn, SCAN_NG, SCAN_GL, SCAN_GL), _F32)
            o, zf = _wkv_scan(*scan_in, z0, chunk, DEC_SCAN_BS if decode else SEQ_SCAN_BS)
            new_wkv.append(_bd_to_state(zf))
            cprev = convb_s[e] if decode else jnp.zeros((bn, B_CONV - 1, B_WIDTH), _F32)
            xx = jnp.concatenate([cprev, u], axis=1)
            new_convb.append(xx[:, t:])
            if decode:
                tp = _CONV_HALO
                mo = _mix_out(_pad_rows(o[:, :t], tp), _pad_rows(bon, tp), _pad_rows(gg, tp),
                              _pad_rows(xx, tp + _CONV_HALO), W, e, tp)[:, :t]
            else:
                mo = _mix_out(o, bon, gg, _pad_rows(xx, t + _CONV_HALO), W, e, SEQ_TM_MIX)
            xf = _post_mm(mo.reshape(r, -1), (W['ab_w_out'], e), xf, W['g_post_mix'][l], modv(l, 2),
                          tm, POST_TK, bpb)
        else:
            oi = l // 2
            n_qkv = W['attn_w_qkv'].shape[2]
            qkv, = _premix(xf, W['g_pre_mix'][l], modv(l, 0), modv(l, 1), (W['attn_w_qkv'], oi),
                           n_qkv, tm, QKV_TN, bpb, emit_h=False)
            q5 = qkv.reshape(bn, t, 3, C_GROUPS, C_HEADS, C_HEAD)
            for gi in range(C_GROUPS):
                keep = min(C_WINDOWS[gi], t)
                new_kv[gi].append(jnp.stack([q5[:, t - keep:, 1, gi], q5[:, t - keep:, 2, gi]]))
            if decode:
                att = _attn_dec(q5, kv_s, oi).reshape(r, C_WIDTH)
            else:
                att = _attn_seq(qkv.reshape(bn, t, n_qkv)).reshape(r, C_WIDTH)
            xf = _post_mm(att, (W['attn_w_out'], oi), xf, W['g_post_mix'][l], modv(l, 2), tm, POST_TK, bpb)
        if decode:
            st = ffn_s[l]
            f = st.shape[-1]
            p1 = jnp.zeros((bn, t, f), _F32).at[:, 0].set(st[:, 1]).reshape(r, f)
            p2 = jnp.zeros((bn, t, f), _F32).at[:, 0].set(st[:, 0]).at[:, 1].set(st[:, 1]).reshape(r, f)
            xf, gt = _ffn(xf, W['g_pre_ffn'][l], modv(l, 3), modv(l, 4), (W['ffn_w_gate'], l),
                          (W['ffn_w_up'], l), W['ffn_conv_w'][l], W['ffn_conv_b'][l], (W['ffn_w_down'], l),
                          W['g_post_ffn'][l], modv(l, 5), tm, FFN_TF, bpb, prev=(p1, p2), seg=t)
            new_ffn.append(gt.reshape(bn, t, f)[:, t - 2:])
        else:
            xf, gt = _ffn(xf, W['g_pre_ffn'][l], modv(l, 3), modv(l, 4), (W['ffn_w_gate'], l),
                          (W['ffn_w_up'], l), W['ffn_conv_w'][l], W['ffn_conv_b'][l], (W['ffn_w_down'], l),
                          W['g_post_ffn'][l], modv(l, 5), tm, FFN_TF, bpb)
            f = gt.shape[-1]
            new_ffn.append(gt.reshape(bn, bpb, SUBLANES, f)[:, -1, SUBLANES - 2:])
    return (xf.reshape(bn, t, d), jnp.stack(new_shift), jnp.stack(new_wkv), jnp.stack(new_convb),
            jnp.stack(new_ffn), jnp.stack(new_kv[0]), jnp.stack(new_kv[1]), jnp.stack(new_kv[2]))


def kernel(x_prompt, x_sample, state_shift, state_wkv, state_conv_b, state_ffn, cache_kv_w128, cache_kv_w512, cache_kv_w2048, c_prompt, c_sample, w_mod, b_mod, g_pre_mix, g_post_mix, g_pre_ffn, g_post_ffn, ab_w_in, a_mu_rkv, a_mu_wag, a_w0, a_w1, a_w2, a_a0, a_a1, a_a2, a_g1, a_g2, a_k_k, a_k_a, a_r_k, a_ln_w, a_ln_b, b_conv_w, b_conv_b, b_ln_w, b_ln_b, ab_w_out, attn_w_qkv, attn_w_out, ffn_w_gate, ffn_w_up, ffn_conv_w, ffn_conv_b, ffn_w_down):
    W = dict(w_mod=w_mod, b_mod=b_mod, g_pre_mix=g_pre_mix, g_post_mix=g_post_mix,
             g_pre_ffn=g_pre_ffn, g_post_ffn=g_post_ffn, ab_w_in=ab_w_in, a_mu_rkv=a_mu_rkv,
             a_mu_wag=a_mu_wag, a_w0=a_w0, a_w1=a_w1, a_w2=a_w2, a_a0=a_a0, a_a1=a_a1, a_a2=a_a2,
             a_g1=a_g1, a_g2=a_g2, a_k_k=a_k_k, a_k_a=a_k_a, a_r_k=a_r_k, a_ln_w=a_ln_w, a_ln_b=a_ln_b,
             b_conv_w=b_conv_w, b_conv_b=b_conv_b, b_ln_w=b_ln_w, b_ln_b=b_ln_b, ab_w_out=ab_w_out,
             attn_w_qkv=attn_w_qkv, attn_w_out=attn_w_out, ffn_w_gate=ffn_w_gate, ffn_w_up=ffn_w_up,
             ffn_conv_w=ffn_conv_w, ffn_conv_b=ffn_conv_b, ffn_w_down=ffn_w_down)
    nbp = x_prompt.shape[0]
    nbs = x_sample.shape[0]
    d = x_prompt.shape[-1]
    assert d == D_MODEL and w_mod.shape == (DEPTH, d, N_MOD * d)
    c_all = _pad_rows(jnp.concatenate([c_prompt, c_sample], axis=0), 16, axis=0)
    mods = _mods(c_all, w_mod, b_mod).reshape(DEPTH, 16, N_MOD, d)
    outs_p = _trunk(x_prompt, mods[:, :nbp], None, None, None, None, None, W, decode=False)
    outs_s = _trunk(x_sample, mods[:, nbp:nbp + nbs], state_shift, state_wkv, state_conv_b,
                    state_ffn, (cache_kv_w128, cache_kv_w512, cache_kv_w2048), W, decode=True)
    (y_p, p_shift, p_wkv, p_conv_b, p_ffn, p_kv0, p_kv1, p_kv2) = outs_p
    (y_s, s_shift, s_wkv, s_conv_b, s_ffn, s_kv0, s_kv1, s_kv2) = outs_s
    return (y_p, y_s, p_shift, p_wkv, p_conv_b, p_ffn, p_kv0, p_kv1, p_kv2,
            s_shift, s_wkv, s_conv_b, s_ffn, s_kv0, s_kv1, s_kv2)
```
